```python
import jax, jax.numpy as jnp
from jax import lax
import numpy as np

D_MODEL = 1024
BATCH = 8
SEQ = 4096
DEPTH = 2

N_A = DEPTH // 2
N_B = DEPTH - N_A
N_META = 16
HEAD_DIM = 64
N_HEADS_A = D_MODEL // HEAD_DIM
LORA_DECAY = 64
LORA_AAA = 64
LORA_GATE = 128
GN_EPS = 64e-5
N_HEADS_Q = D_MODEL // HEAD_DIM
N_HEADS_KV = 4
GROUP = N_HEADS_Q // N_HEADS_KV
WINDOW = 128
BLOCK = 128
PAD_FRONT = BLOCK - N_META
ROPE_THETA = 10000.0
D_FF = 4 * D_MODEL
ALPHA = (2.0 * DEPTH) ** 0.25
BETA = (8.0 * DEPTH) ** -0.25
LN_EPS = 1e-5

kernel_name = "yoco_rwkv7_swa_sink_hybrid"


def layer_norm(x, g, b):
    xf = x.astype(jnp.float32)
    mu = jnp.mean(xf, axis=-1, keepdims=True)
    xc = xf - mu
    var = jnp.mean(xc * xc, axis=-1, keepdims=True)
    y = xc * lax.rsqrt(var + LN_EPS) * g.astype(jnp.float32) + b.astype(jnp.float32)
    return y.astype(x.dtype)


def rope_tables(length):
    inv_freq = 1.0 / (ROPE_THETA ** (jnp.arange(0, HEAD_DIM, 2, dtype=jnp.float32) / HEAD_DIM))
    ang = jnp.arange(length, dtype=jnp.float32)[:, None] * inv_freq[None, :]
    return jnp.cos(ang), jnp.sin(ang)


def apply_rope(t, cos, sin):
    tf = t.astype(jnp.float32)
    t1, t2 = jnp.split(tf, 2, axis=-1)
    c = cos[None, :, None, :]
    s = sin[None, :, None, :]
    out = jnp.concatenate([t1 * c - t2 * s, t2 * c + t1 * s], axis=-1)
    return out.astype(t.dtype)


def sq_relu_mlp(x, w_up, w_down):
    h = jax.nn.relu(x @ w_up)
    return (h * h) @ w_down


def rwkv7_time_mix(x, mu, w_r, w_k, w_v, w_o, w0, w1, w2, a0, a1, a2, g1, g2,
                   k_k, k_a, r_k, gn_w, gn_b):
    bsz, length, d = x.shape
    H, N = N_HEADS_A, HEAD_DIM
    x_prev = jnp.pad(x, ((0, 0), (1, 0), (0, 0)))[:, :-1]
    xx = x_prev - x
    xr = x + xx * mu[0]
    xw = x + xx * mu[1]
    xk = x + xx * mu[2]
    xv = x + xx * mu[3]
    xa = x + xx * mu[4]
    xg = x + xx * mu[5]
    r = xr @ w_r
    k = xk @ w_k
    v = xv @ w_v
    w = -jax.nn.softplus(-(w0 + jnp.tanh(xw @ w1) @ w2)) - 0.5
    a = jax.nn.sigmoid(a0 + (xa @ a1) @ a2)
    g = jax.nn.sigmoid(xg @ g1) @ g2
    f32 = jnp.float32
    kk = (k * k_k).reshape(bsz, length, H, N).astype(f32)
    kk = kk / jnp.maximum(jnp.linalg.norm(kk, axis=-1, keepdims=True), 1e-12)
    k = k * (1.0 + (a - 1.0) * k_a)
    rh = r.reshape(bsz, length, H, N).astype(f32)
    kh = k.reshape(bsz, length, H, N).astype(f32)
    vh = v.reshape(bsz, length, H, N).astype(f32)
    ah = a.reshape(bsz, length, H, N).astype(f32)
    decay = jnp.exp(-jnp.exp(w.reshape(bsz, length, H, N).astype(f32)))
    seq_first = lambda t: jnp.transpose(t, (1, 0, 2, 3))
    inputs = (seq_first(rh), seq_first(decay), seq_first(kh), seq_first(vh),
              seq_first(-kk), seq_first(kk * ah))

    def step(state, inp):
        r_t, w_t, k_t, v_t, a_t, b_t = inp
        sa = jnp.einsum('bhij,bhj->bhi', state, a_t)
        state = (state * w_t[:, :, None, :] + sa[..., None] * b_t[:, :, None, :]
                 + v_t[..., None] * k_t[:, :, None, :])
        y = jnp.einsum('bhij,bhj->bhi', state, r_t)
        return state, y

    state0 = jnp.zeros((bsz, H, N, N), f32)
    _, ys = lax.scan(step, state0, inputs)
    y = jnp.transpose(ys, (1, 0, 2, 3))
    ym = jnp.mean(y, axis=-1, keepdims=True)
    yc = y - ym
    yv = jnp.mean(yc * yc, axis=-1, keepdims=True)
    y = (yc * lax.rsqrt(yv + GN_EPS) * gn_w.reshape(H, N).astype(f32)
         + gn_b.reshape(H, N).astype(f32))
    bonus = jnp.sum(rh * kh * r_k.astype(f32), axis=-1, keepdims=True) * vh
    o = (y + bonus).reshape(bsz, length, d).astype(x.dtype) * g
    return o @ w_o


def to_blocks(t):
    pad = [(0, 0), (PAD_FRONT, 0)] + [(0, 0)] * (t.ndim - 2)
    t = jnp.pad(t, pad)
    return t.reshape(t.shape[0], -1, BLOCK, *t.shape[2:])


def with_prev_block(tb):
    pad = [(0, 0), (1, 0)] + [(0, 0)] * (tb.ndim - 2)
    prev = jnp.pad(tb, pad)[:, :-1]
    return jnp.concatenate([prev, tb], axis=2)


def band_mask(nb):
    qi = jnp.arange(BLOCK)[:, None]
    kj = jnp.arange(2 * BLOCK)[None, :]
    rel = BLOCK + qi - kj
    in_window = (rel >= 0) & (rel < WINDOW)
    key_pos = (jnp.arange(nb)[:, None] - 1) * BLOCK + jnp.arange(2 * BLOCK)[None, :]
    valid = key_pos >= PAD_FRONT
    return in_window[None] & valid[:, None, :]


def shared_kv(x, w_k, w_v, cos, sin):
    bsz, length, _ = x.shape
    k = (x @ w_k).reshape(bsz, length, N_HEADS_KV, HEAD_DIM)
    v = (x @ w_v).reshape(bsz, length, N_HEADS_KV, HEAD_DIM)
    k = apply_rope(k, cos, sin)
    return with_prev_block(to_blocks(k)), with_prev_block(to_blocks(v))


def swa_sink_attention(x, w_q, sinks, w_o, kb, vb, cos, sin, mask):
    bsz, length, d = x.shape
    q = (x @ w_q).reshape(bsz, length, N_HEADS_Q, HEAD_DIM)
    q = apply_rope(q, cos, sin)
    qb = to_blocks(q)
    nb = qb.shape[1]
    qb = qb.reshape(bsz, nb, BLOCK, N_HEADS_KV, GROUP, HEAD_DIM)
    f32 = jnp.float32
    s = jnp.einsum('bnqhgd,bnshd->bnhgqs', qb.astype(f32), kb.astype(f32)) * (HEAD_DIM ** -0.5)
    s = jnp.where(mask[None, :, None, None], s, -jnp.inf)
    sink = sinks.reshape(N_HEADS_KV, GROUP).astype(f32)[None, None, :, :, None]
    m = jnp.maximum(jnp.max(s, axis=-1), sink)
    p = jnp.exp(s - m[..., None])
    denom = jnp.sum(p, axis=-1) + jnp.exp(sink - m)
    p = p / denom[..., None]
    o = jnp.einsum('bnhgqs,bnshd->bnqhgd', p, vb.astype(f32))
    o = o.reshape(bsz, nb * BLOCK, d)[:, PAD_FRONT:].astype(x.dtype)
    return o @ w_o


def setup_inputs(seed: int = 0) -> dict:
    key = jax.random.key(seed)
    ks = iter(jax.random.split(key, 40))
    D = D_MODEL
    inv = D ** -0.5

    def nrm(shape, scale):
        return jax.random.normal(next(ks), shape, jnp.float32) * scale

    return {
        "x": nrm((BATCH, SEQ, D), 1.0),
        "meta_tokens": nrm((N_META, D), 1.0),
        "a_mu": jax.random.uniform(next(ks), (N_A, 6, D), jnp.float32),
        "a_w_r": nrm((N_A, D, D), inv),
        "a_w_k": nrm((N_A, D, D), inv),
        "a_w_v": nrm((N_A, D, D), inv * BETA),
        "a_w_o": nrm((N_A, D, D), inv * BETA),
        "a_w0": jax.random.uniform(next(ks), (N_A, D), jnp.float32, -6.0, 1.0),
        "a_w1": nrm((N_A, D, LORA_DECAY), inv),
        "a_w2": nrm((N_A, LORA_DECAY, D), 0.1 * LORA_DECAY ** -0.5),
        "a_a0": nrm((N_A, D), 0.5),
        "a_a1": nrm((N_A, D, LORA_AAA), inv),
        "a_a2": nrm((N_A, LORA_AAA, D), LORA_AAA ** -0.5),
        "a_g1": nrm((N_A, D, LORA_GATE), inv),
        "a_g2": nrm((N_A, LORA_GATE, D), LORA_GATE ** -0.5),
        "a_k_k": 0.85 + nrm((N_A, D), 0.05),
        "a_k_a": 1.0 + nrm((N_A, D), 0.05),
        "a_r_k": nrm((N_A, N_HEADS_A, HEAD_DIM), 0.1),
        "a_gn_w": 1.0 + nrm((N_A, D), 0.05),
        "a_gn_b": nrm((N_A, D), 0.02),
        "kv_w_k": nrm((D, N_HEADS_KV * HEAD_DIM), inv),
        "kv_w_v": nrm((D, N_HEADS_KV * HEAD_DIM), inv * BETA),
        "b_w_q": nrm((N_B, D, N_HEADS_Q * HEAD_DIM), inv),
        "b_sinks": nrm((N_B, N_HEADS_Q), 1.0),
        "b_w_o": nrm((N_B, N_HEADS_Q * HEAD_DIM, D), inv * BETA),
        "mlp_w_up": nrm((DEPTH, D, D_FF), inv),
        "mlp_w_down": nrm((DEPTH, D_FF, D), D_FF ** -0.5 * BETA),
        "ln_g": 1.0 + nrm((DEPTH, 2, D), 0.05),
        "ln_b": nrm((DEPTH, 2, D), 0.02),
    }


def reference(x, meta_tokens, a_mu, a_w_r, a_w_k, a_w_v, a_w_o, a_w0, a_w1, a_w2,
              a_a0, a_a1, a_a2, a_g1, a_g2, a_k_k, a_k_a, a_r_k, a_gn_w, a_gn_b,
              kv_w_k, kv_w_v, b_w_q, b_sinks, b_w_o, mlp_w_up, mlp_w_down, ln_g, ln_b):
    bsz = x.shape[0]
    meta = jnp.broadcast_to(meta_tokens[None].astype(x.dtype), (bsz, N_META, x.shape[2]))
    h = jnp.concatenate([meta, x], axis=1)
    length = h.shape[1]
    cos, sin = rope_tables(length)
    nb = (length + PAD_FRONT) // BLOCK
    mask = band_mask(nb)
    kb = vb = None
    for i in range(DEPTH):
        if i < N_A:
            j = i
            mix = rwkv7_time_mix(h, a_mu[j], a_w_r[j], a_w_k[j], a_w_v[j], a_w_o[j],
                                 a_w0[j], a_w1[j], a_w2[j], a_a0[j], a_a1[j], a_a2[j],
                                 a_g1[j], a_g2[j], a_k_k[j], a_k_a[j], a_r_k[j],
                                 a_gn_w[j], a_gn_b[j])
        else:
            if i == N_A:
                kb, vb = shared_kv(h, kv_w_k, kv_w_v, cos, sin)
            j = i - N_A
            mix = swa_sink_attention(h, b_w_q[j], b_sinks[j], b_w_o[j], kb, vb, cos, sin, mask)
        h = layer_norm(ALPHA * h + mix, ln_g[i, 0], ln_b[i, 0])
        h = layer_norm(ALPHA * h + sq_relu_mlp(h, mlp_w_up[i], mlp_w_down[i]), ln_g[i, 1], ln_b[i, 1])
    return h[:, N_META:]
```

```python
import functools

import jax
import jax.numpy as jnp
from jax import lax
from jax.experimental import pallas as pl
from jax.experimental.pallas import tpu as pltpu

D_MODEL = 1024
N_META = 16
HEAD_DIM = 64
N_HEADS = D_MODEL // HEAD_DIM
N_HEADS_KV = 4
GROUP = N_HEADS // N_HEADS_KV
LORA_DECAY = 64
LORA_AAA = 64
LORA_GATE = 128
GN_EPS = 64e-5
WINDOW = 128
BLOCK = 128
PAD_FRONT = BLOCK - N_META
ROPE_THETA = 10000.0
D_FF = 4 * D_MODEL
DEPTH = 2
ALPHA = (2.0 * DEPTH) ** 0.25
LN_EPS = 1e-5

CHUNK = 64
TM_ROWS = 512
TM_ATT = 384
FF_CHUNK = 1024
VMEM_LIMIT = 56 * 1024 * 1024

BF16 = jnp.bfloat16
F32 = jnp.float32


def _dot(a, b):
    return jnp.dot(a, b, preferred_element_type=F32)


def _dot_nt(a, b):
    return lax.dot_general(a, b, (((1,), (1,)), ((), ())), preferred_element_type=F32)


def _dot_tn(a, b):
    return lax.dot_general(a, b, (((0,), (0,)), ((), ())), preferred_element_type=F32)


def _layer_norm(x, g, b):
    mu = jnp.mean(x, axis=-1, keepdims=True)
    xc = x - mu
    var = jnp.mean(xc * xc, axis=-1, keepdims=True)
    return xc * lax.rsqrt(var + LN_EPS) * g + b


def _deepnorm_mlp(h, mix, ln_ref, wup_ref, wdn_ref):
    h1 = _layer_norm(ALPHA * h + mix, ln_ref[0:1, :], ln_ref[1:2, :])
    hb = h1.astype(BF16)
    acc = jnp.zeros_like(h1)
    for c in range(D_FF // FF_CHUNK):
        up = _dot(hb, wup_ref[:, c * FF_CHUNK:(c + 1) * FF_CHUNK])
        up = jnp.maximum(up, 0.0)
        acc = acc + _dot((up * up).astype(BF16), wdn_ref[c * FF_CHUNK:(c + 1) * FF_CHUNK, :])
    return _layer_norm(ALPHA * h1 + acc, ln_ref[2:3, :], ln_ref[3:4, :])


def _proj_kernel(seq_len, h_ref, hprev_ref, mu_ref, vec_ref, wr_ref, wk_ref, wv_ref,
                 w1_ref, w2_ref, a1_ref, a2_ref, g1_ref, g2_ref, gsum_ref, gexp_ref,
                 r_out, lw_out, k_out, v_out, kk_out, b_out, g_out):
    x = h_ref[...]
    tm = x.shape[0]
    row = lax.broadcasted_iota(jnp.int32, x.shape, 0)
    tile_start = pl.program_id(0) * tm
    seq_start = (tile_start + seq_len - 1) // seq_len * seq_len - tile_start
    prev_row = hprev_ref[7:8, :]
    x_prev = jnp.where(row == 0, prev_row, pltpu.roll(x, 1, axis=0))
    x_prev = jnp.where(row == seq_start, 0.0, x_prev)
    xx = x_prev - x

    def mix(i):
        return (x + xx * mu_ref[i:i + 1, :]).astype(BF16)

    r = _dot(mix(0), wr_ref[...])
    k = _dot(mix(2), wk_ref[...])
    v = _dot(mix(3), wv_ref[...])
    w_lora = _dot(jnp.tanh(_dot(mix(1), w1_ref[...])).astype(BF16), w2_ref[...])
    a_lora = _dot(_dot(mix(4), a1_ref[...]).astype(BF16), a2_ref[...])
    g = _dot(jax.nn.sigmoid(_dot(mix(5), g1_ref[...])).astype(BF16), g2_ref[...])

    w0 = vec_ref[0:1, :]
    a0 = vec_ref[1:2, :]
    k_k = vec_ref[2:3, :]
    k_a = vec_ref[3:4, :]
    w = -jax.nn.softplus(-(w0 + w_lora)) - 0.5
    lw = -jnp.exp(w)
    a = jax.nn.sigmoid(a0 + a_lora)

    kk = k * k_k
    ss = _dot((kk * kk).astype(BF16), gsum_ref[...])
    inv = lax.rsqrt(jnp.maximum(ss, 1e-24))
    inv_hi = inv.astype(BF16)
    inv_lo = (inv - inv_hi.astype(F32)).astype(BF16)
    inv_full = _dot(jnp.concatenate([inv_hi, inv_lo], axis=1), gexp_ref[...])
    kk = kk * inv_full

    r_out[...] = r
    lw_out[...] = lw
    k_out[...] = k * (1.0 + (a - 1.0) * k_a)
    v_out[...] = v
    kk_out[...] = kk
    b_out[...] = kk * a
    g_out[...] = g


def _rwkv_proj(h_flat, seq_len, mu, vec, wr, wk, wv, w1, w2, a1, a2, g1, g2, gsum, gexp):
    rows = h_flat.shape[0]
    tm = TM_ROWS
    const = lambda shape: pl.BlockSpec(shape, lambda i: (0, 0), pipeline_mode=pl.Buffered(1))
    tile = pl.BlockSpec((tm, D_MODEL), lambda i: (i, 0))
    out_sds = jax.ShapeDtypeStruct((rows, D_MODEL), F32)
    return pl.pallas_call(
        functools.partial(_proj_kernel, seq_len),
        grid=(rows // tm,),
        in_specs=[
            tile,
            pl.BlockSpec((8, D_MODEL), lambda i: (jnp.maximum(i * (tm // 8) - 1, 0), 0)),
            const(mu.shape), const(vec.shape), const(wr.shape), const(wk.shape), const(wv.shape),
            const(w1.shape), const(w2.shape), const(a1.shape), const(a2.shape),
            const(g1.shape), const(g2.shape), const(gsum.shape), const(gexp.shape),
        ],
        out_specs=[tile] * 7,
        out_shape=[out_sds] * 7,
        compiler_params=pltpu.CompilerParams(
            dimension_semantics=("parallel",), vmem_limit_bytes=VMEM_LIMIT),
        name="rwkv_proj",
    )(h_flat, h_flat, mu, vec, wr, wk, wv, w1, w2, a1, a2, g1, g2, gsum, gexp)


def _scan_kernel(r_ref, lw_ref, k_ref, v_ref, kk_ref, b_ref, g_ref, hv_ref, o_ref, s_ref):
    C = CHUNK

    @pl.when(pl.program_id(1) == 0)
    def _():
        s_ref[...] = jnp.zeros_like(s_ref)

    ri = lax.broadcasted_iota(jnp.int32, (C, C), 0)
    ci = lax.broadcasted_iota(jnp.int32, (C, C), 1)
    incl = ri >= ci
    strict = ri > ci
    eye = (ri == ci).astype(F32)

    lw = lw_ref[0]
    c = jnp.dot(incl.astype(F32), lw, preferred_element_type=F32,
                precision=lax.Precision.HIGHEST)
    c_last = c[C - 1:C, :]
    e_c = jnp.exp(c)
    e_nc = jnp.exp(-c)
    e_cc = jnp.exp(c_last - c)
    e_cp = jnp.exp(c - lw)
    dec = jnp.exp(c_last)

    r = r_ref[0]
    k = k_ref[0]
    v = v_ref[0]
    kk = kk_ref[0]
    b = b_ref[0]
    a_t = (-kk * e_cp).astype(BF16)
    r_t = (r * e_c).astype(BF16)
    k_h = (k * e_nc).astype(BF16)
    b_h = (b * e_nc).astype(BF16)
    k_b = (k * e_cc).astype(BF16)
    b_b = (b * e_cc).astype(BF16)
    v_b = v.astype(BF16)
    rk = r * k * hv_ref[0:1, :]

    for h in range(N_HEADS):
        sl = slice(h * HEAD_DIM, (h + 1) * HEAD_DIM)
        at, rt, kh, bh, kb, bb, vh = (t[:, sl] for t in (a_t, r_t, k_h, b_h, k_b, b_b, v_b))
        a_ab = jnp.where(strict, _dot_nt(at, bh), 0.0)
        a_ak = jnp.where(strict, _dot_nt(at, kh), 0.0)
        a_rb = jnp.where(incl, _dot_nt(rt, bh), 0.0)
        a_rk = jnp.where(incl, _dot_nt(rt, kh), 0.0)
        npow = a_ab.astype(BF16)
        tinv = eye + a_ab
        for _ in range(5):
            nsq = _dot(npow, npow)
            npow = nsq.astype(BF16)
            tinv = tinv + _dot(tinv.astype(BF16), npow)
        s0 = s_ref[h]
        s0b = s0.astype(BF16)
        z = _dot_nt(at, s0b) + _dot(a_ak.astype(BF16), vh)
        u = _dot(tinv.astype(BF16), z.astype(BF16)).astype(BF16)
        y = _dot_nt(rt, s0b) + _dot(a_rk.astype(BF16), vh) + _dot(a_rb.astype(BF16), u)
        s_ref[h] = s0 * dec[:, sl] + _dot_tn(vh, kb) + _dot_tn(u, bb)

        ym = jnp.mean(y, axis=-1, keepdims=True)
        yc = y - ym
        yv = jnp.mean(yc * yc, axis=-1, keepdims=True)
        yn = yc * lax.rsqrt(yv + GN_EPS) * hv_ref[1:2, sl] + hv_ref[2:3, sl]
        bonus = jnp.sum(rk[:, sl], axis=-1, keepdims=True) * v[:, sl]
        o_ref[0, :, sl] = ((yn + bonus) * g_ref[0, :, sl]).astype(o_ref.dtype)


def _rwkv_scan(r, lw, k, v, kk, b, g, hv):
    bsz, lp, d = r.shape
    tile = pl.BlockSpec((1, CHUNK, d), lambda bi, n: (bi, n, 0))
    return pl.pallas_call(
        _scan_kernel,
        grid=(bsz, lp // CHUNK),
        in_specs=[tile] * 7 + [pl.BlockSpec(hv.shape, lambda bi, n: (0, 0))],
        out_specs=tile,
        out_shape=jax.ShapeDtypeStruct((bsz, lp, d), BF16),
        scratch_shapes=[pltpu.VMEM((N_HEADS, HEAD_DIM, HEAD_DIM), F32)],
        compiler_params=pltpu.CompilerParams(
            dimension_semantics=("parallel", "arbitrary"), vmem_limit_bytes=VMEM_LIMIT),
        name="rwkv_scan",
    )(r, lw, k, v, kk, b, g, hv)


def _post_mlp_kernel(o_ref, h_ref, wo_ref, ln_ref, wup_ref, wdn_ref, out_ref):
    mix = _dot(o_ref[...], wo_ref[...])
    out_ref[...] = _deepnorm_mlp(h_ref[...], mix, ln_ref, wup_ref, wdn_ref)


def _post_mlp(o_flat, h_flat, wo, ln, wup, wdn):
    rows = h_flat.shape[0]
    tm = TM_ROWS
    const = lambda shape: pl.BlockSpec(shape, lambda i: (0, 0), pipeline_mode=pl.Buffered(1))
    tile = pl.BlockSpec((tm, D_MODEL), lambda i: (i, 0))
    return pl.pallas_call(
        _post_mlp_kernel,
        grid=(rows // tm,),
        in_specs=[tile, tile, const(wo.shape), const(ln.shape), const(wup.shape), const(wdn.shape)],
        out_specs=tile,
        out_shape=jax.ShapeDtypeStruct((rows, D_MODEL), F32),
        compiler_params=pltpu.CompilerParams(
            dimension_semantics=("parallel",), vmem_limit_bytes=VMEM_LIMIT),
        name="post_mlp",
    )(o_flat, h_flat, wo, ln, wup, wdn)


def _rope(t, cos, sin_signed):
    out = []
    lane = lax.broadcasted_iota(jnp.int32, (t.shape[0], 128), 1)
    first_half = (lane % HEAD_DIM) < (HEAD_DIM // 2)
    for j in range(t.shape[1] // 128):
        tj = t[:, j * 128:(j + 1) * 128]
        rot = jnp.where(first_half, pltpu.roll(tj, 128 - HEAD_DIM // 2, axis=1),
                        pltpu.roll(tj, HEAD_DIM // 2, axis=1))
        out.append(tj * cos + rot * sin_signed)
    return jnp.concatenate(out, axis=1)


def _swa_kernel(sink_ref, h_ref, hprev_ref, cos_ref, sin_ref, cosp_ref, sinp_ref,
                wq_ref, wkv_ref, wo_ref, ln_ref, wup_ref, wdn_ref, out_ref, o_scr):
    h = h_ref[0]
    tm = h.shape[0]
    hb = h.astype(BF16)
    hcat = jnp.concatenate([hprev_ref[0].astype(BF16), hb], axis=0)
    cos_cat = jnp.concatenate([cosp_ref[...], cos_ref[...]], axis=0)
    sin_cat = jnp.concatenate([sinp_ref[...], sin_ref[...]], axis=0)

    q = _rope(_dot(hb, wq_ref[...]), cos_ref[...], sin_ref[...]) * (HEAD_DIM ** -0.5)
    kv = _dot(hcat, wkv_ref[...])
    kdim = N_HEADS_KV * HEAD_DIM
    kr = _rope(kv[:, :kdim], cos_cat, sin_cat).astype(BF16)
    vb = kv[:, kdim:].astype(BF16)
    qb = q.astype(BF16)

    first_row = pl.program_id(1) * tm
    qi = lax.broadcasted_iota(jnp.int32, (BLOCK, 2 * BLOCK), 0)
    kj = lax.broadcasted_iota(jnp.int32, (BLOCK, 2 * BLOCK), 1)
    rel = BLOCK + qi - kj
    in_window = (rel >= 0) & (rel < WINDOW)

    for j in range(tm // BLOCK):
        key_row = first_row + (j - 1) * BLOCK + kj
        mask = in_window & (key_row >= PAD_FRONT)
        for hd in range(N_HEADS):
            g = hd // GROUP
            qh = qb[j * BLOCK:(j + 1) * BLOCK, hd * HEAD_DIM:(hd + 1) * HEAD_DIM]
            kg = kr[j * BLOCK:(j + 2) * BLOCK, g * HEAD_DIM:(g + 1) * HEAD_DIM]
            vg = vb[j * BLOCK:(j + 2) * BLOCK, g * HEAD_DIM:(g + 1) * HEAD_DIM]
            s = jnp.where(mask, _dot_nt(qh, kg), -jnp.inf)
            sink = sink_ref[hd]
            m = jnp.maximum(jnp.max(s, axis=-1, keepdims=True), sink)
            p = jnp.exp(s - m)
            denom = jnp.sum(p, axis=-1, keepdims=True) + jnp.exp(sink - m)
            oh = _dot(p.astype(BF16), vg) / denom
            o_scr[j * BLOCK:(j + 1) * BLOCK, hd * HEAD_DIM:(hd + 1) * HEAD_DIM] = oh.astype(BF16)

    mix = _dot(o_scr[...], wo_ref[...])
    out_ref[0] = _deepnorm_mlp(h, mix, ln_ref, wup_ref, wdn_ref)


def _swa_layer(h, sinks, cos, sin_signed, wq, wkv, wo, ln, wup, wdn):
    bsz, lp, d = h.shape
    tm = TM_ATT
    nblk = tm // BLOCK
    const = lambda shape: pl.BlockSpec(shape, lambda bi, i: (0, 0), pipeline_mode=pl.Buffered(1))
    prev = lambda bi, i: (bi, jnp.maximum(i * nblk - 1, 0), 0)
    prev2 = lambda bi, i: (jnp.maximum(i * nblk - 1, 0), 0)
    return pl.pallas_call(
        _swa_kernel,
        grid=(bsz, lp // tm),
        in_specs=[
            pl.BlockSpec(memory_space=pltpu.SMEM),
            pl.BlockSpec((1, tm, d), lambda bi, i: (bi, i, 0)),
            pl.BlockSpec((1, BLOCK, d), prev),
            pl.BlockSpec((tm, 128), lambda bi, i: (i, 0)),
            pl.BlockSpec((tm, 128), lambda bi, i: (i, 0)),
            pl.BlockSpec((BLOCK, 128), prev2),
            pl.BlockSpec((BLOCK, 128), prev2),
            const(wq.shape), const(wkv.shape), const(wo.shape), const(ln.shape),
            const(wup.shape), const(wdn.shape),
        ],
        out_specs=pl.BlockSpec((1, tm, d), lambda bi, i: (bi, i, 0)),
        out_shape=jax.ShapeDtypeStruct((bsz, lp, d), F32),
        scratch_shapes=[pltpu.VMEM((tm, d), BF16)],
        compiler_params=pltpu.CompilerParams(
            dimension_semantics=("parallel", "parallel"), vmem_limit_bytes=VMEM_LIMIT),
        name="swa_layer",
    )(sinks, h, h, cos, sin_signed, cos, sin_signed, wq, wkv, wo, ln, wup, wdn)


def kernel(x, meta_tokens, a_mu, a_w_r, a_w_k, a_w_v, a_w_o, a_w0, a_w1, a_w2, a_a0, a_a1, a_a2,
           a_g1, a_g2, a_k_k, a_k_a, a_r_k, a_gn_w, a_gn_b, kv_w_k, kv_w_v, b_w_q, b_sinks, b_w_o,
           mlp_w_up, mlp_w_down, ln_g, ln_b):
    bsz, seq, d = x.shape
    lp = PAD_FRONT + N_META + seq
    bf = lambda t: t.astype(BF16)

    meta = jnp.broadcast_to(meta_tokens[None].astype(x.dtype), (bsz, N_META, d))
    h0 = jnp.concatenate([jnp.zeros((bsz, PAD_FRONT, d), x.dtype), meta, x], axis=1)
    h0_flat = h0.reshape(bsz * lp, d)

    head_of_lane = jnp.arange(d) // HEAD_DIM
    gsum = (head_of_lane[:, None] == jnp.arange(128)[None, :]).astype(BF16)
    gexp = jnp.concatenate([gsum.T, gsum.T], axis=0)

    vec = jnp.stack([a_w0[0], a_a0[0], a_k_k[0], a_k_a[0]])
    r, lw, k, v, kk, b, g = _rwkv_proj(
        h0_flat, lp, a_mu[0], vec, bf(a_w_r[0]), bf(a_w_k[0]), bf(a_w_v[0]),
        bf(a_w1[0]), bf(a_w2[0]), bf(a_a1[0]), bf(a_a2[0]), bf(a_g1[0]), bf(a_g2[0]), gsum, gexp)

    hv = jnp.stack([a_r_k[0].reshape(d), a_gn_w[0], a_gn_b[0]])
    shp = (bsz, lp, d)
    o = _rwkv_scan(*(t.reshape(shp) for t in (r, lw, k, v, kk, b, g)), hv)

    ln0 = jnp.stack([ln_g[0, 0], ln_b[0, 0], ln_g[0, 1], ln_b[0, 1]])
    h1 = _post_mlp(o.reshape(bsz * lp, d), h0_flat, bf(a_w_o[0]), ln0,
                   bf(mlp_w_up[0]), bf(mlp_w_down[0]))

    pos = (jnp.arange(lp) - PAD_FRONT).astype(F32)
    inv_freq = 1.0 / (ROPE_THETA ** (jnp.arange(0, HEAD_DIM, 2, dtype=F32) / HEAD_DIM))
    ang = pos[:, None] * inv_freq[None, :]
    cos = jnp.tile(jnp.cos(ang), (1, 4))
    sin_signed = jnp.tile(jnp.concatenate([-jnp.sin(ang), jnp.sin(ang)], axis=1), (1, 2))

    ln1 = jnp.stack([ln_g[1, 0], ln_b[1, 0], ln_g[1, 1], ln_b[1, 1]])
    wkv = bf(jnp.concatenate([kv_w_k, kv_w_v], axis=1))
    h2 = _swa_layer(h1.reshape(shp), b_sinks[0], cos, sin_signed, bf(b_w_q[0]), wkv, bf(b_w_o[0]),
                    ln1, bf(mlp_w_up[1]), bf(mlp_w_down[1]))
    return h2[:, PAD_FRONT + N_META:]
```

```python
import functools

import jax
import jax.numpy as jnp
from jax import lax
from jax.experimental import pallas as pl
from jax.experimental.pallas import tpu as pltpu

D_MODEL = 1024
N_META = 16
HEAD_DIM = 64
N_HEADS = D_MODEL // HEAD_DIM
N_HEADS_KV = 4
GROUP = N_HEADS // N_HEADS_KV
GN_EPS = 64e-5
WINDOW = 128
BLOCK = 128
PAD_FRONT = BLOCK - N_META
ROPE_THETA = 10000.0
D_FF = 4 * D_MODEL
DEPTH = 2
ALPHA = (2.0 * DEPTH) ** 0.25
LN_EPS = 1e-5

LANES = 128
N_PAIRS = D_MODEL // LANES
CHUNK = 64
TM_ROWS = 512
TA_ROWS = 384
TM_ATT = 384
FF_CHUNK = 1024
VMEM_LIMIT = 56 * 1024 * 1024

BF16 = jnp.bfloat16
F32 = jnp.float32


def _dot(a, b):
    return jnp.dot(a, b, preferred_element_type=F32)


def _dot_nt(a, b):
    return lax.dot_general(a, b, (((1,), (1,)), ((), ())), preferred_element_type=F32)


def _dot_tn(a, b):
    return lax.dot_general(a, b, (((0,), (0,)), ((), ())), preferred_element_type=F32)


def _split_bf16(x):
    hi = x.astype(BF16)
    lo = (x - hi.astype(F32)).astype(BF16)
    return jnp.concatenate([hi, lo], axis=1)


def _block_diag(x):
    first = lax.broadcasted_iota(jnp.int32, x.shape, 1) < HEAD_DIM
    zero = jnp.zeros_like(x)
    return jnp.concatenate([jnp.where(first, x, zero), jnp.where(first, zero, x)], axis=0)


def _diag_blocks(full):
    first = lax.broadcasted_iota(jnp.int32, (HEAD_DIM, LANES), 1) < HEAD_DIM
    return jnp.where(first, full[:HEAD_DIM], full[HEAD_DIM:])


def _layer_norm(x, g, b):
    mu = jnp.mean(x, axis=-1, keepdims=True)
    xc = x - mu
    var = jnp.mean(xc * xc, axis=-1, keepdims=True)
    return xc * lax.rsqrt(var + LN_EPS) * g + b


def _deepnorm_mlp(h, mix, ln_ref, wup_ref, wdn_ref):
    h1 = _layer_norm(ALPHA * h + mix, ln_ref[0:1, :], ln_ref[1:2, :])
    hb = h1.astype(BF16)
    acc = jnp.zeros_like(h1)
    for c in range(D_FF // FF_CHUNK):
        up = _dot(hb, wup_ref[:, c * FF_CHUNK:(c + 1) * FF_CHUNK])
        up = jnp.maximum(up, 0.0)
        acc = acc + _dot((up * up).astype(BF16), wdn_ref[c * FF_CHUNK:(c + 1) * FF_CHUNK, :])
    return _layer_norm(ALPHA * h1 + acc, ln_ref[2:3, :], ln_ref[3:4, :])


def _proj_kernel(seq_len, h_ref, hprev_ref, mu_ref, vec_ref, wr_ref, wk_ref, wv_ref,
                 w1_ref, w2_ref, a1_ref, a2_ref, g1_ref, g2_ref, gsum_ref, gexp_ref,
                 r_out, lw_out, k_out, v_out, kk_out, b_out, g_out, bonus_out):
    x = h_ref[...]
    tm = x.shape[0]
    row = lax.broadcasted_iota(jnp.int32, x.shape, 0)
    tile_start = pl.program_id(0) * tm
    seq_start = (tile_start + seq_len - 1) // seq_len * seq_len - tile_start
    prev_row = hprev_ref[7:8, :]
    x_prev = jnp.where(row == 0, prev_row, pltpu.roll(x, 1, axis=0))
    x_prev = jnp.where(row == seq_start, 0.0, x_prev)
    xx = x_prev - x

    def mix(i):
        return (x + xx * mu_ref[i:i + 1, :]).astype(BF16)

    r = _dot(mix(0), wr_ref[...])
    k = _dot(mix(2), wk_ref[...])
    v = _dot(mix(3), wv_ref[...])
    w_lora = _dot(jnp.tanh(_dot(mix(1), w1_ref[...])).astype(BF16), w2_ref[...])
    a_lora = _dot(_dot(mix(4), a1_ref[...]).astype(BF16), a2_ref[...])
    g = _dot(jax.nn.sigmoid(_dot(mix(5), g1_ref[...])).astype(BF16), g2_ref[...])

    w0 = vec_ref[0:1, :]
    a0 = vec_ref[1:2, :]
    k_k = vec_ref[2:3, :]
    k_a = vec_ref[3:4, :]
    r_k = vec_ref[4:5, :]
    w = -jax.nn.softplus(-(w0 + w_lora)) - 0.5
    lw = -jnp.exp(w)
    a = jax.nn.sigmoid(a0 + a_lora)

    kk = k * k_k
    ss = _dot((kk * kk).astype(BF16), gsum_ref[...])
    inv = lax.rsqrt(jnp.maximum(ss, 1e-24))
    kk = kk * _dot(_split_bf16(inv), gexp_ref[...])
    k2 = k * (1.0 + (a - 1.0) * k_a)
    rk_sum = _dot((r * k2 * r_k).astype(BF16), gsum_ref[...])

    r_out[...] = r
    lw_out[...] = lw
    k_out[...] = k2
    v_out[...] = v
    kk_out[...] = kk
    b_out[...] = kk * a
    g_out[...] = g.astype(g_out.dtype)
    bonus_out[...] = (_dot(_split_bf16(rk_sum), gexp_ref[...]) * v).astype(bonus_out.dtype)


def _rwkv_proj(h_flat, seq_len, mu, vec, wr, wk, wv, w1, w2, a1, a2, g1, g2, gsum, gexp):
    rows = h_flat.shape[0]
    tm = TM_ROWS
    const = lambda shape: pl.BlockSpec(shape, lambda i: (0, 0), pipeline_mode=pl.Buffered(1))
    tile = pl.BlockSpec((tm, D_MODEL), lambda i: (i, 0))
    f32_sds = jax.ShapeDtypeStruct((rows, D_MODEL), F32)
    bf16_sds = jax.ShapeDtypeStruct((rows, D_MODEL), BF16)
    return pl.pallas_call(
        functools.partial(_proj_kernel, seq_len),
        grid=(rows // tm,),
        in_specs=[
            tile,
            pl.BlockSpec((8, D_MODEL), lambda i: (jnp.maximum(i * (tm // 8) - 1, 0), 0)),
            const(mu.shape), const(vec.shape), const(wr.shape), const(wk.shape), const(wv.shape),
            const(w1.shape), const(w2.shape), const(a1.shape), const(a2.shape),
            const(g1.shape), const(g2.shape), const(gsum.shape), const(gexp.shape),
        ],
        out_specs=[tile] * 8,
        out_shape=[f32_sds] * 6 + [bf16_sds] * 2,
        compiler_params=pltpu.CompilerParams(
            dimension_semantics=("parallel",), vmem_limit_bytes=VMEM_LIMIT),
        name="rwkv_proj",
    )(h_flat, h_flat, mu, vec, wr, wk, wv, w1, w2, a1, a2, g1, g2, gsum, gexp)


def _chunk_kernel(r_ref, lw_ref, k_ref, v_ref, kk_ref, b_ref,
                  rw_out, y0_out, mlr_out, sadd_out, dec_out,
                  at_s, rt_s, kh_s, bh_s, kb_s, bb_s, v_s):
    C = CHUNK
    ta = lw_ref.shape[1]
    nch = ta // C

    lw = lw_ref[0]
    row_in_chunk = lax.broadcasted_iota(jnp.int32, lw.shape, 0) & (C - 1)
    c = lw
    for s in (1, 2, 4, 8, 16, 32):
        c = c + jnp.where(row_in_chunk >= s, pltpu.roll(c, s, axis=0), 0.0)
    c3 = c.reshape(nch, C, D_MODEL)
    c_last = c3[:, C - 1:C, :]
    e_cc = jnp.exp(c_last - c3).reshape(ta, D_MODEL)
    dec_out[0] = jnp.exp(c_last)
    e_nc = jnp.exp(-c)
    k = k_ref[0]
    b = b_ref[0]
    at_s[...] = (-kk_ref[0] * jnp.exp(c - lw)).astype(BF16)
    rt_s[...] = (r_ref[0] * jnp.exp(c)).astype(BF16)
    kh_s[...] = (k * e_nc).astype(BF16)
    bh_s[...] = (b * e_nc).astype(BF16)
    kb_s[...] = (k * e_cc).astype(BF16)
    bb_s[...] = (b * e_cc).astype(BF16)
    v_s[...] = v_ref[0].astype(BF16)

    t_idx = lax.broadcasted_iota(jnp.int32, (C, LANES), 0)
    s_idx = lax.broadcasted_iota(jnp.int32, (C, LANES), 1) & (HEAD_DIM - 1)
    strict = s_idx < t_idx
    incl = s_idx <= t_idx
    eye = (s_idx == t_idx).astype(F32)

    def chunk_body(j, carry):
        rows = pl.ds(pl.multiple_of(j * C, C), C)
        for p in range(N_PAIRS):
            ls = slice(p * LANES, (p + 1) * LANES)
            at, rt, kh, bh, kb, bb, vv = (t[rows, ls] for t in
                                          (at_s, rt_s, kh_s, bh_s, kb_s, bb_s, v_s))
            ar = jnp.concatenate([at, rt], axis=0)
            gk = _dot_nt(ar, _block_diag(kh))
            gb = _dot_nt(ar, _block_diag(bh))
            a_ak = jnp.where(strict, gk[:C], 0.0).astype(BF16)
            a_rk = jnp.where(incl, gk[C:], 0.0).astype(BF16)
            a_ab = jnp.where(strict, gb[:C], 0.0)
            a_rb = jnp.where(incl, gb[C:], 0.0).astype(BF16)
            tinv = eye + a_ab
            npow = a_ab.astype(BF16)
            npow = _dot(npow, _block_diag(npow)).astype(BF16)
            for _ in range(4):
                res = _dot(jnp.concatenate([npow, tinv.astype(BF16)], axis=0), _block_diag(npow))
                tinv = tinv + res[C:]
                npow = res[:C].astype(BF16)
            tinv = tinv + _dot(tinv.astype(BF16), _block_diag(npow))
            vbd = _block_diag(vv)
            x = _dot(a_ak, vbd).astype(BF16)
            wu = _dot(tinv.astype(BF16), jnp.concatenate([_block_diag(at), _block_diag(x)], axis=1))
            w_b = wu[:, :LANES].astype(BF16)
            u0_b = wu[:, LANES:].astype(BF16)
            rw = rt.astype(F32) + _dot(a_rb, _block_diag(w_b))
            y0 = _dot(jnp.concatenate([a_rk, a_rb], axis=1),
                      jnp.concatenate([vbd, _block_diag(u0_b)], axis=0))
            mlr = _diag_blocks(_dot_tn(w_b, bb))
            sadd = _diag_blocks(_dot_tn(jnp.concatenate([vv, u0_b], axis=0),
                                        jnp.concatenate([kb, bb], axis=0)))
            rw_out[0, rows, ls] = rw.astype(rw_out.dtype)
            y0_out[0, rows, ls] = y0
            mlr_out[0, j, :, ls] = mlr.astype(mlr_out.dtype)
            sadd_out[0, j, :, ls] = sadd
        return carry

    lax.fori_loop(0, nch, chunk_body, 0)


def _rwkv_chunk(r, lw, k, v, kk, b):
    bsz, lp, d = r.shape
    ta = TA_ROWS
    nch = ta // CHUNK
    nc = lp // CHUNK
    tile = pl.BlockSpec((1, ta, d), lambda bi, i: (bi, i, 0))
    per_chunk = lambda rows: pl.BlockSpec((1, nch, rows, d), lambda bi, i: (bi, i, 0, 0))
    return pl.pallas_call(
        _chunk_kernel,
        grid=(bsz, lp // ta),
        in_specs=[tile] * 6,
        out_specs=[tile, tile, per_chunk(HEAD_DIM), per_chunk(HEAD_DIM), per_chunk(1)],
        out_shape=[
            jax.ShapeDtypeStruct((bsz, lp, d), BF16),
            jax.ShapeDtypeStruct((bsz, lp, d), F32),
            jax.ShapeDtypeStruct((bsz, nc, HEAD_DIM, d), BF16),
            jax.ShapeDtypeStruct((bsz, nc, HEAD_DIM, d), F32),
            jax.ShapeDtypeStruct((bsz, nc, 1, d), F32),
        ],
        scratch_shapes=[pltpu.VMEM((ta, d), BF16)] * 7,
        compiler_params=pltpu.CompilerParams(
            dimension_semantics=("parallel", "parallel"), vmem_limit_bytes=VMEM_LIMIT),
        name="rwkv_chunk",
    )(r, lw, k, v, kk, b)


def _state_kernel(rw_ref, y0_ref, mlr_ref, sadd_ref, dec_ref, bonus_ref, g_ref, hv_ref,
                  gsum_ref, gexp_ref, o_ref, s_ref, y_s):
    bsz = rw_ref.shape[0]

    @pl.when(pl.program_id(0) == 0)
    def _():
        s_ref[...] = jnp.zeros_like(s_ref)

    def seq_body(bi, carry):
        for p in range(N_PAIRS):
            ls = slice(p * LANES, (p + 1) * LANES)
            s0 = s_ref[bi, :, ls]
            s0b = s0.astype(BF16)
            y_s[bi, :, ls] = y0_ref[bi, :, ls] + _dot_nt(rw_ref[bi, :, ls], _block_diag(s0b))
            s_ref[bi, :, ls] = (s0 * dec_ref[bi, 0, :, ls]
                                + _dot(s0b, _block_diag(mlr_ref[bi, 0, :, ls]))
                                + sadd_ref[bi, 0, :, ls])
        return carry

    lax.fori_loop(0, bsz, seq_body, 0)

    y = y_s[...].reshape(bsz * CHUNK, D_MODEL)
    inv_n = 1.0 / HEAD_DIM
    mean = _dot(_split_bf16(_dot(y.astype(BF16), gsum_ref[...]) * inv_n), gexp_ref[...])
    yc = y - mean
    var = _dot((yc * yc).astype(BF16), gsum_ref[...]) * inv_n
    rstd = _dot(_split_bf16(lax.rsqrt(var + GN_EPS)), gexp_ref[...])
    yn = yc * rstd * hv_ref[0:1, :] + hv_ref[1:2, :]
    bonus = bonus_ref[...].reshape(bsz * CHUNK, D_MODEL).astype(F32)
    g = g_ref[...].reshape(bsz * CHUNK, D_MODEL).astype(F32)
    o_ref[...] = ((yn + bonus) * g).astype(o_ref.dtype).reshape(o_ref.shape)


def _rwkv_state(rw, y0, mlr, sadd, dec, bonus, g, hv, gsum, gexp):
    bsz, lp, d = rw.shape
    nc = lp // CHUNK
    tile = pl.BlockSpec((bsz, CHUNK, d), lambda n: (0, n, 0))
    per_chunk = lambda rows: pl.BlockSpec((bsz, 1, rows, d), lambda n: (0, n, 0, 0))
    const = lambda shape: pl.BlockSpec(shape, lambda n: (0, 0))
    return pl.pallas_call(
        _state_kernel,
        grid=(nc,),
        in_specs=[tile, tile, per_chunk(HEAD_DIM), per_chunk(HEAD_DIM), per_chunk(1), tile, tile,
                  const(hv.shape), const(gsum.shape), const(gexp.shape)],
        out_specs=tile,
        out_shape=jax.ShapeDtypeStruct((bsz, lp, d), BF16),
        scratch_shapes=[pltpu.VMEM((bsz, HEAD_DIM, d), F32), pltpu.VMEM((bsz, CHUNK, d), F32)],
        compiler_params=pltpu.CompilerParams(
            dimension_semantics=("arbitrary",), vmem_limit_bytes=VMEM_LIMIT),
        name="rwkv_state",
    )(rw, y0, mlr, sadd, dec, bonus, g, hv, gsum, gexp)


def _post_mlp_kernel(o_ref, h_ref, wo_ref, ln_ref, wup_ref, wdn_ref, out_ref):
    mix = _dot(o_ref[...], wo_ref[...])
    out_ref[...] = _deepnorm_mlp(h_ref[...], mix, ln_ref, wup_ref, wdn_ref)


def _post_mlp(o_flat, h_flat, wo, ln, wup, wdn):
    rows = h_flat.shape[0]
    tm = TM_ROWS
    const = lambda shape: pl.BlockSpec(shape, lambda i: (0, 0), pipeline_mode=pl.Buffered(1))
    tile = pl.BlockSpec((tm, D_MODEL), lambda i: (i, 0))
    return pl.pallas_call(
        _post_mlp_kernel,
        grid=(rows // tm,),
        in_specs=[tile, tile, const(wo.shape), const(ln.shape), const(wup.shape), const(wdn.shape)],
        out_specs=tile,
        out_shape=jax.ShapeDtypeStruct((rows, D_MODEL), F32),
        compiler_params=pltpu.CompilerParams(
            dimension_semantics=("parallel",), vmem_limit_bytes=VMEM_LIMIT),
        name="post_mlp",
    )(o_flat, h_flat, wo, ln, wup, wdn)


def _rope(t, cos, sin_signed):
    out = []
    lane = lax.broadcasted_iota(jnp.int32, (t.shape[0], LANES), 1)
    first_half = (lane % HEAD_DIM) < (HEAD_DIM // 2)
    for j in range(t.shape[1] // LANES):
        tj = t[:, j * LANES:(j + 1) * LANES]
        rot = jnp.where(first_half, pltpu.roll(tj, LANES - HEAD_DIM // 2, axis=1),
                        pltpu.roll(tj, HEAD_DIM // 2, axis=1))
        out.append(tj * cos + rot * sin_signed)
    return jnp.concatenate(out, axis=1)


def _swa_kernel(sink_ref, h_ref, hprev_ref, cos_ref, sin_ref, cosp_ref, sinp_ref,
                wq_ref, wkv_ref, wo_ref, ln_ref, wup_ref, wdn_ref, out_ref, o_scr):
    h = h_ref[0]
    tm = h.shape[0]
    hb = h.astype(BF16)
    hcat = jnp.concatenate([hprev_ref[0].astype(BF16), hb], axis=0)
    cos_cat = jnp.concatenate([cosp_ref[...], cos_ref[...]], axis=0)
    sin_cat = jnp.concatenate([sinp_ref[...], sin_ref[...]], axis=0)

    q = _rope(_dot(hb, wq_ref[...]), cos_ref[...], sin_ref[...]) * (HEAD_DIM ** -0.5)
    kv = _dot(hcat, wkv_ref[...])
    kdim = N_HEADS_KV * HEAD_DIM
    kr = _rope(kv[:, :kdim], cos_cat, sin_cat).astype(BF16)
    vb = kv[:, kdim:].astype(BF16)
    qb = q.astype(BF16)

    first_row = pl.program_id(1) * tm
    qi = lax.broadcasted_iota(jnp.int32, (BLOCK, 2 * BLOCK), 0)
    kj = lax.broadcasted_iota(jnp.int32, (BLOCK, 2 * BLOCK), 1)
    rel = BLOCK + qi - kj
    in_window = (rel >= 0) & (rel < WINDOW)

    for j in range(tm // BLOCK):
        key_row = first_row + (j - 1) * BLOCK + kj
        mask = in_window & (key_row >= PAD_FRONT)
        for hd in range(N_HEADS):
            g = hd // GROUP
            qh = qb[j * BLOCK:(j + 1) * BLOCK, hd * HEAD_DIM:(hd + 1) * HEAD_DIM]
            kg = kr[j * BLOCK:(j + 2) * BLOCK, g * HEAD_DIM:(g + 1) * HEAD_DIM]
            vg = vb[j * BLOCK:(j + 2) * BLOCK, g * HEAD_DIM:(g + 1) * HEAD_DIM]
            s = jnp.where(mask, _dot_nt(qh, kg), -jnp.inf)
            sink = sink_ref[hd]
            m = jnp.maximum(jnp.max(s, axis=-1, keepdims=True), sink)
            p = jnp.exp(s - m)
            denom = jnp.sum(p, axis=-1, keepdims=True) + jnp.exp(sink - m)
            oh = _dot(p.astype(BF16), vg) / denom
            o_scr[j * BLOCK:(j + 1) * BLOCK, hd * HEAD_DIM:(hd + 1) * HEAD_DIM] = oh.astype(BF16)

    mix = _dot(o_scr[...], wo_ref[...])
    out_ref[0] = _deepnorm_mlp(h, mix, ln_ref, wup_ref, wdn_ref)


def _swa_layer(h, sinks, cos, sin_signed, wq, wkv, wo, ln, wup, wdn):
    bsz, lp, d = h.shape
    tm = TM_ATT
    nblk = tm // BLOCK
    const = lambda shape: pl.BlockSpec(shape, lambda bi, i: (0, 0), pipeline_mode=pl.Buffered(1))
    prev = lambda bi, i: (bi, jnp.maximum(i * nblk - 1, 0), 0)
    prev2 = lambda bi, i: (jnp.maximum(i * nblk - 1, 0), 0)
    return pl.pallas_call(
        _swa_kernel,
        grid=(bsz, lp // tm),
        in_specs=[
            pl.BlockSpec(memory_space=pltpu.SMEM),
            pl.BlockSpec((1, tm, d), lambda bi, i: (bi, i, 0)),
            pl.BlockSpec((1, BLOCK, d), prev),
            pl.BlockSpec((tm, LANES), lambda bi, i: (i, 0)),
            pl.BlockSpec((tm, LANES), lambda bi, i: (i, 0)),
            pl.BlockSpec((BLOCK, LANES), prev2),
            pl.BlockSpec((BLOCK, LANES), prev2),
            const(wq.shape), const(wkv.shape), const(wo.shape), const(ln.shape),
            const(wup.shape), const(wdn.shape),
        ],
        out_specs=pl.BlockSpec((1, tm, d), lambda bi, i: (bi, i, 0)),
        out_shape=jax.ShapeDtypeStruct((bsz, lp, d), F32),
        scratch_shapes=[pltpu.VMEM((tm, d), BF16)],
        compiler_params=pltpu.CompilerParams(
            dimension_semantics=("parallel", "parallel"), vmem_limit_bytes=VMEM_LIMIT),
        name="swa_layer",
    )(sinks, h, h, cos, sin_signed, cos, sin_signed, wq, wkv, wo, ln, wup, wdn)


def kernel(x, meta_tokens, a_mu, a_w_r, a_w_k, a_w_v, a_w_o, a_w0, a_w1, a_w2, a_a0, a_a1, a_a2,
           a_g1, a_g2, a_k_k, a_k_a, a_r_k, a_gn_w, a_gn_b, kv_w_k, kv_w_v, b_w_q, b_sinks, b_w_o,
           mlp_w_up, mlp_w_down, ln_g, ln_b):
    bsz, seq, d = x.shape
    lp = PAD_FRONT + N_META + seq
    bf = lambda t: t.astype(BF16)

    meta = jnp.broadcast_to(meta_tokens[None].astype(x.dtype), (bsz, N_META, d))
    h0 = jnp.concatenate([jnp.zeros((bsz, PAD_FRONT, d), x.dtype), meta, x], axis=1)
    h0_flat = h0.reshape(bsz * lp, d)

    head_of_lane = jnp.arange(d) // HEAD_DIM
    gsum = (head_of_lane[:, None] == jnp.arange(LANES)[None, :]).astype(BF16)
    gexp = jnp.concatenate([gsum.T, gsum.T], axis=0)

    vec = jnp.stack([a_w0[0], a_a0[0], a_k_k[0], a_k_a[0], a_r_k[0].reshape(d)])
    r, lw, k, v, kk, b, g, bonus = _rwkv_proj(
        h0_flat, lp, a_mu[0], vec, bf(a_w_r[0]), bf(a_w_k[0]), bf(a_w_v[0]),
        bf(a_w1[0]), bf(a_w2[0]), bf(a_a1[0]), bf(a_a2[0]), bf(a_g1[0]), bf(a_g2[0]), gsum, gexp)

    shp = (bsz, lp, d)
    rw, y0, mlr, sadd, dec = _rwkv_chunk(*(t.reshape(shp) for t in (r, lw, k, v, kk, b)))
    hv = jnp.stack([a_gn_w[0], a_gn_b[0]])
    o = _rwkv_state(rw, y0, mlr, sadd, dec, bonus.reshape(shp), g.reshape(shp), hv, gsum, gexp)

    ln0 = jnp.stack([ln_g[0, 0], ln_b[0, 0], ln_g[0, 1], ln_b[0, 1]])
    h1 = _post_mlp(o.reshape(bsz * lp, d), h0_flat, bf(a_w_o[0]), ln0,
                   bf(mlp_w_up[0]), bf(mlp_w_down[0]))

    pos = (jnp.arange(lp) - PAD_FRONT).astype(F32)
    inv_freq = 1.0 / (ROPE_THETA ** (jnp.arange(0, HEAD_DIM, 2, dtype=F32) / HEAD_DIM))
    ang = pos[:, None] * inv_freq[None, :]
    cos = jnp.tile(jnp.cos(ang), (1, 4))
    sin_signed = jnp.tile(jnp.concatenate([-jnp.sin(ang), jnp.sin(ang)], axis=1), (1, 2))

    ln1 = jnp.stack([ln_g[1, 0], ln_b[1, 0], ln_g[1, 1], ln_b[1, 1]])
    wkv = bf(jnp.concatenate([kv_w_k, kv_w_v], axis=1))
    h2 = _swa_layer(h1.reshape(shp), b_sinks[0], cos, sin_signed, bf(b_w_q[0]), wkv, bf(b_w_o[0]),
                    ln1, bf(mlp_w_up[1]), bf(mlp_w_down[1]))
    return h2[:, PAD_FRONT + N_META:]
```

```python
import functools

import jax
import jax.numpy as jnp
from jax import lax
from jax.experimental import pallas as pl
from jax.experimental.pallas import tpu as pltpu

D_MODEL = 1024
N_META = 16
HEAD_DIM = 64
N_HEADS = D_MODEL // HEAD_DIM
N_HEADS_KV = 4
GROUP = N_HEADS // N_HEADS_KV
GN_EPS = 64e-5
WINDOW = 128
BLOCK = 128
PAD_FRONT = BLOCK - N_META
ROPE_THETA = 10000.0
D_FF = 4 * D_MODEL
DEPTH = 2
ALPHA = (2.0 * DEPTH) ** 0.25
LN_EPS = 1e-5

LANES = 128
N_PAIRS = D_MODEL // LANES
CHUNK = 64
TM_ROWS = 512
TA_ROWS = 384
TM_ATT = 384
FF_CHUNK = 1024
VMEM_LIMIT = 56 * 1024 * 1024

BF16 = jnp.bfloat16
F32 = jnp.float32


def _dot(a, b):
    return jnp.dot(a, b, preferred_element_type=F32)


def _dot_nt(a, b):
    return lax.dot_general(a, b, (((1,), (1,)), ((), ())), preferred_element_type=F32)


def _dot_tn(a, b):
    return lax.dot_general(a, b, (((0,), (0,)), ((), ())), preferred_element_type=F32)


def _split_bf16(x):
    hi = x.astype(BF16)
    lo = (x - hi.astype(F32)).astype(BF16)
    return jnp.concatenate([hi, lo], axis=1)


def _block_diag(x):
    first = lax.broadcasted_iota(jnp.int32, x.shape, 1) < HEAD_DIM
    zero = jnp.zeros_like(x)
    return jnp.concatenate([jnp.where(first, x, zero), jnp.where(first, zero, x)], axis=0)


def _diag_blocks(full):
    first = lax.broadcasted_iota(jnp.int32, (HEAD_DIM, LANES), 1) < HEAD_DIM
    return jnp.where(first, full[:HEAD_DIM], full[HEAD_DIM:])


def _layer_norm(x, g, b):
    mu = jnp.mean(x, axis=-1, keepdims=True)
    xc = x - mu
    var = jnp.mean(xc * xc, axis=-1, keepdims=True)
    return xc * lax.rsqrt(var + LN_EPS) * g + b


def _deepnorm_mlp(h, mix, ln_ref, wup_ref, wdn_ref):
    h1 = _layer_norm(ALPHA * h + mix, ln_ref[0:1, :], ln_ref[1:2, :])
    hb = h1.astype(BF16)
    acc = jnp.zeros_like(h1)
    for c in range(D_FF // FF_CHUNK):
        up = _dot(hb, wup_ref[:, c * FF_CHUNK:(c + 1) * FF_CHUNK])
        up = jnp.maximum(up, 0.0)
        acc = acc + _dot((up * up).astype(BF16), wdn_ref[c * FF_CHUNK:(c + 1) * FF_CHUNK, :])
    return _layer_norm(ALPHA * h1 + acc, ln_ref[2:3, :], ln_ref[3:4, :])


def _proj_kernel(seq_len, h_ref, hprev_ref, mu_ref, vec_ref, wr_ref, wk_ref, wv_ref,
                 w1_ref, w2_ref, a1_ref, a2_ref, g1_ref, g2_ref, gsum_ref, gexp_ref,
                 r_out, lw_out, k_out, v_out, kk_out, b_out, g_out, bonus_out):
    x = h_ref[...]
    tm = x.shape[0]
    row = lax.broadcasted_iota(jnp.int32, x.shape, 0)
    tile_start = pl.program_id(0) * tm
    seq_start = (tile_start + seq_len - 1) // seq_len * seq_len - tile_start
    prev_row = hprev_ref[7:8, :]
    x_prev = jnp.where(row == 0, prev_row, pltpu.roll(x, 1, axis=0))
    x_prev = jnp.where(row == seq_start, 0.0, x_prev)
    xx = x_prev - x

    def mix(i):
        return (x + xx * mu_ref[i:i + 1, :]).astype(BF16)

    r = _dot(mix(0), wr_ref[...])
    k = _dot(mix(2), wk_ref[...])
    v = _dot(mix(3), wv_ref[...])
    w_lora = _dot(jnp.tanh(_dot(mix(1), w1_ref[...])).astype(BF16), w2_ref[...])
    a_lora = _dot(_dot(mix(4), a1_ref[...]).astype(BF16), a2_ref[...])
    g = _dot(jax.nn.sigmoid(_dot(mix(5), g1_ref[...])).astype(BF16), g2_ref[...])

    w0 = vec_ref[0:1, :]
    a0 = vec_ref[1:2, :]
    k_k = vec_ref[2:3, :]
    k_a = vec_ref[3:4, :]
    r_k = vec_ref[4:5, :]
    w = -jax.nn.softplus(-(w0 + w_lora)) - 0.5
    lw = -jnp.exp(w)
    a = jax.nn.sigmoid(a0 + a_lora)

    kk = k * k_k
    ss = _dot((kk * kk).astype(BF16), gsum_ref[...])
    inv = lax.rsqrt(jnp.maximum(ss, 1e-24))
    kk = kk * _dot(_split_bf16(inv), gexp_ref[...])
    k2 = k * (1.0 + (a - 1.0) * k_a)
    rk_sum = _dot((r * k2 * r_k).astype(BF16), gsum_ref[...])

    r_out[...] = r
    lw_out[...] = lw
    k_out[...] = k2
    v_out[...] = v
    kk_out[...] = kk
    b_out[...] = kk * a
    g_out[...] = g.astype(g_out.dtype)
    bonus_out[...] = (_dot(_split_bf16(rk_sum), gexp_ref[...]) * v).astype(bonus_out.dtype)


def _rwkv_proj(h_flat, seq_len, mu, vec, wr, wk, wv, w1, w2, a1, a2, g1, g2, gsum, gexp):
    rows = h_flat.shape[0]
    tm = TM_ROWS
    const = lambda shape: pl.BlockSpec(shape, lambda i: (0, 0), pipeline_mode=pl.Buffered(1))
    tile = pl.BlockSpec((tm, D_MODEL), lambda i: (i, 0))
    f32_sds = jax.ShapeDtypeStruct((rows, D_MODEL), F32)
    bf16_sds = jax.ShapeDtypeStruct((rows, D_MODEL), BF16)
    return pl.pallas_call(
        functools.partial(_proj_kernel, seq_len),
        grid=(rows // tm,),
        in_specs=[
            tile,
            pl.BlockSpec((8, D_MODEL), lambda i: (jnp.maximum(i * (tm // 8) - 1, 0), 0)),
            const(mu.shape), const(vec.shape), const(wr.shape), const(wk.shape), const(wv.shape),
            const(w1.shape), const(w2.shape), const(a1.shape), const(a2.shape),
            const(g1.shape), const(g2.shape), const(gsum.shape), const(gexp.shape),
        ],
        out_specs=[tile] * 8,
        out_shape=[f32_sds] * 6 + [bf16_sds] * 2,
        compiler_params=pltpu.CompilerParams(
            dimension_semantics=("parallel",), vmem_limit_bytes=VMEM_LIMIT),
        name="rwkv_proj",
    )(h_flat, h_flat, mu, vec, wr, wk, wv, w1, w2, a1, a2, g1, g2, gsum, gexp)


def _chunk_kernel(r_ref, lw_ref, k_ref, v_ref, kk_ref, b_ref,
                  rw_out, y0_out, mlr_out, sadd_out, dec_out,
                  at_s, rt_s, kh_s, bh_s, kb_s, bb_s, v_s):
    C = CHUNK
    ta = lw_ref.shape[1]
    nch = ta // C

    lw = lw_ref[0]
    row_in_chunk = lax.broadcasted_iota(jnp.int32, lw.shape, 0) & (C - 1)
    c = lw
    for s in (1, 2, 4, 8, 16, 32):
        c = c + jnp.where(row_in_chunk >= s, pltpu.roll(c, s, axis=0), 0.0)
    c3 = c.reshape(nch, C, D_MODEL)
    c_last = c3[:, C - 1:C, :]
    e_cc = jnp.exp(c_last - c3).reshape(ta, D_MODEL)
    dec_out[0] = jnp.exp(c_last)
    e_nc = jnp.exp(-c)
    k = k_ref[0]
    b = b_ref[0]
    at_s[...] = (-kk_ref[0] * jnp.exp(c - lw)).astype(BF16)
    rt_s[...] = (r_ref[0] * jnp.exp(c)).astype(BF16)
    kh_s[...] = (k * e_nc).astype(BF16)
    bh_s[...] = (b * e_nc).astype(BF16)
    kb_s[...] = (k * e_cc).astype(BF16)
    bb_s[...] = (b * e_cc).astype(BF16)
    v_s[...] = v_ref[0].astype(BF16)

    t_idx = lax.broadcasted_iota(jnp.int32, (C, LANES), 0)
    s_idx = lax.broadcasted_iota(jnp.int32, (C, LANES), 1) & (HEAD_DIM - 1)
    strict = s_idx < t_idx
    incl = s_idx <= t_idx
    eye = (s_idx == t_idx).astype(F32)

    def chunk_body(j, carry):
        rows = pl.ds(pl.multiple_of(j * C, C), C)
        pairs = range(N_PAIRS)
        lanes = [slice(p * LANES, (p + 1) * LANES) for p in pairs]
        at = [at_s[rows, ls] for ls in lanes]
        rt = [rt_s[rows, ls] for ls in lanes]
        ar = [jnp.concatenate([at[p], rt[p]], axis=0) for p in pairs]
        gk = [_dot_nt(ar[p], _block_diag(kh_s[rows, lanes[p]])) for p in pairs]
        gb = [_dot_nt(ar[p], _block_diag(bh_s[rows, lanes[p]])) for p in pairs]
        a_ak = [jnp.where(strict, gk[p][:C], 0.0).astype(BF16) for p in pairs]
        a_rk = [jnp.where(incl, gk[p][C:], 0.0).astype(BF16) for p in pairs]
        a_rb = [jnp.where(incl, gb[p][C:], 0.0).astype(BF16) for p in pairs]
        a_ab = [jnp.where(strict, gb[p][:C], 0.0) for p in pairs]
        tinv = [eye + a_ab[p] for p in pairs]
        npow = [a_ab[p].astype(BF16) for p in pairs]
        npow = [_dot(npow[p], _block_diag(npow[p])).astype(BF16) for p in pairs]
        for _ in range(4):
            res = [_dot(jnp.concatenate([npow[p], tinv[p].astype(BF16)], axis=0),
                        _block_diag(npow[p])) for p in pairs]
            tinv = [tinv[p] + res[p][C:] for p in pairs]
            npow = [res[p][:C].astype(BF16) for p in pairs]
        tinv = [(tinv[p] + _dot(tinv[p].astype(BF16), _block_diag(npow[p]))).astype(BF16)
                for p in pairs]
        vv = [v_s[rows, ls] for ls in lanes]
        x = [_dot(a_ak[p], _block_diag(vv[p])).astype(BF16) for p in pairs]
        wu = [_dot(tinv[p], jnp.concatenate([_block_diag(at[p]), _block_diag(x[p])], axis=1))
              for p in pairs]
        w_b = [wu[p][:, :LANES].astype(BF16) for p in pairs]
        u0_b = [wu[p][:, LANES:].astype(BF16) for p in pairs]
        for p in pairs:
            rw = rt[p].astype(F32) + _dot(a_rb[p], _block_diag(w_b[p]))
            rw_out[0, rows, lanes[p]] = rw.astype(rw_out.dtype)
        for p in pairs:
            y0_out[0, rows, lanes[p]] = _dot(
                jnp.concatenate([a_rk[p], a_rb[p]], axis=1),
                jnp.concatenate([_block_diag(vv[p]), _block_diag(u0_b[p])], axis=0))
        for p in pairs:
            bb = bb_s[rows, lanes[p]]
            mlr_out[0, j, :, lanes[p]] = _diag_blocks(_dot_tn(w_b[p], bb)).astype(mlr_out.dtype)
            sadd_out[0, j, :, lanes[p]] = _diag_blocks(
                _dot_tn(jnp.concatenate([vv[p], u0_b[p]], axis=0),
                        jnp.concatenate([kb_s[rows, lanes[p]], bb], axis=0)))
        return carry

    lax.fori_loop(0, nch, chunk_body, 0)


def _rwkv_chunk(r, lw, k, v, kk, b):
    bsz, lp, d = r.shape
    ta = TA_ROWS
    nch = ta // CHUNK
    nc = lp // CHUNK
    tile = pl.BlockSpec((1, ta, d), lambda bi, i: (bi, i, 0))
    per_chunk = lambda rows: pl.BlockSpec((1, nch, rows, d), lambda bi, i: (bi, i, 0, 0))
    return pl.pallas_call(
        _chunk_kernel,
        grid=(bsz, lp // ta),
        in_specs=[tile] * 6,
        out_specs=[tile, tile, per_chunk(HEAD_DIM), per_chunk(HEAD_DIM), per_chunk(1)],
        out_shape=[
            jax.ShapeDtypeStruct((bsz, lp, d), BF16),
            jax.ShapeDtypeStruct((bsz, lp, d), F32),
            jax.ShapeDtypeStruct((bsz, nc, HEAD_DIM, d), BF16),
            jax.ShapeDtypeStruct((bsz, nc, HEAD_DIM, d), F32),
            jax.ShapeDtypeStruct((bsz, nc, 1, d), F32),
        ],
        scratch_shapes=[pltpu.VMEM((ta, d), BF16)] * 7,
        compiler_params=pltpu.CompilerParams(
            dimension_semantics=("parallel", "parallel"), vmem_limit_bytes=VMEM_LIMIT),
        name="rwkv_chunk",
    )(r, lw, k, v, kk, b)


def _state_kernel(rw_ref, y0_ref, mlr_ref, sadd_ref, dec_ref, bonus_ref, g_ref, hv_ref,
                  gsum_ref, gexp_ref, o_ref, s_ref, y_s):
    bsz = rw_ref.shape[0]

    @pl.when(pl.program_id(0) == 0)
    def _():
        s_ref[...] = jnp.zeros_like(s_ref)

    def seq_body(bi, carry):
        for p in range(N_PAIRS):
            ls = slice(p * LANES, (p + 1) * LANES)
            s0 = s_ref[bi, :, ls]
            s0b = s0.astype(BF16)
            y_s[bi, :, ls] = y0_ref[bi, :, ls] + _dot_nt(rw_ref[bi, :, ls], _block_diag(s0b))
            s_ref[bi, :, ls] = (s0 * dec_ref[bi, 0, :, ls]
                                + _dot(s0b, _block_diag(mlr_ref[bi, 0, :, ls]))
                                + sadd_ref[bi, 0, :, ls])
        return carry

    lax.fori_loop(0, bsz, seq_body, 0)

    y = y_s[...].reshape(bsz * CHUNK, D_MODEL)
    inv_n = 1.0 / HEAD_DIM
    mean = _dot(_split_bf16(_dot(y.astype(BF16), gsum_ref[...]) * inv_n), gexp_ref[...])
    yc = y - mean
    var = _dot((yc * yc).astype(BF16), gsum_ref[...]) * inv_n
    rstd = _dot(_split_bf16(lax.rsqrt(var + GN_EPS)), gexp_ref[...])
    yn = yc * rstd * hv_ref[0:1, :] + hv_ref[1:2, :]
    bonus = bonus_ref[...].reshape(bsz * CHUNK, D_MODEL).astype(F32)
    g = g_ref[...].reshape(bsz * CHUNK, D_MODEL).astype(F32)
    o_ref[...] = ((yn + bonus) * g).astype(o_ref.dtype).reshape(o_ref.shape)


def _rwkv_state(rw, y0, mlr, sadd, dec, bonus, g, hv, gsum, gexp):
    bsz, lp, d = rw.shape
    nc = lp // CHUNK
    tile = pl.BlockSpec((bsz, CHUNK, d), lambda n: (0, n, 0))
    per_chunk = lambda rows: pl.BlockSpec((bsz, 1, rows, d), lambda n: (0, n, 0, 0))
    const = lambda shape: pl.BlockSpec(shape, lambda n: (0, 0))
    return pl.pallas_call(
        _state_kernel,
        grid=(nc,),
        in_specs=[tile, tile, per_chunk(HEAD_DIM), per_chunk(HEAD_DIM), per_chunk(1), tile, tile,
                  const(hv.shape), const(gsum.shape), const(gexp.shape)],
        out_specs=tile,
        out_shape=jax.ShapeDtypeStruct((bsz, lp, d), BF16),
        scratch_shapes=[pltpu.VMEM((bsz, HEAD_DIM, d), F32), pltpu.VMEM((bsz, CHUNK, d), F32)],
        compiler_params=pltpu.CompilerParams(
            dimension_semantics=("arbitrary",), vmem_limit_bytes=VMEM_LIMIT),
        name="rwkv_state",
    )(rw, y0, mlr, sadd, dec, bonus, g, hv, gsum, gexp)


def _post_mlp_kernel(o_ref, h_ref, wo_ref, ln_ref, wup_ref, wdn_ref, out_ref):
    mix = _dot(o_ref[...], wo_ref[...])
    out_ref[...] = _deepnorm_mlp(h_ref[...], mix, ln_ref, wup_ref, wdn_ref)


def _post_mlp(o_flat, h_flat, wo, ln, wup, wdn):
    rows = h_flat.shape[0]
    tm = TM_ROWS
    const = lambda shape: pl.BlockSpec(shape, lambda i: (0, 0), pipeline_mode=pl.Buffered(1))
    tile = pl.BlockSpec((tm, D_MODEL), lambda i: (i, 0))
    return pl.pallas_call(
        _post_mlp_kernel,
        grid=(rows // tm,),
        in_specs=[tile, tile, const(wo.shape), const(ln.shape), const(wup.shape), const(wdn.shape)],
        out_specs=tile,
        out_shape=jax.ShapeDtypeStruct((rows, D_MODEL), F32),
        compiler_params=pltpu.CompilerParams(
            dimension_semantics=("parallel",), vmem_limit_bytes=VMEM_LIMIT),
        name="post_mlp",
    )(o_flat, h_flat, wo, ln, wup, wdn)


def _rope(t, cos, sin_signed):
    out = []
    lane = lax.broadcasted_iota(jnp.int32, (t.shape[0], LANES), 1)
    first_half = (lane % HEAD_DIM) < (HEAD_DIM // 2)
    for j in range(t.shape[1] // LANES):
        tj = t[:, j * LANES:(j + 1) * LANES]
        rot = jnp.where(first_half, pltpu.roll(tj, LANES - HEAD_DIM // 2, axis=1),
                        pltpu.roll(tj, HEAD_DIM // 2, axis=1))
        out.append(tj * cos + rot * sin_signed)
    return jnp.concatenate(out, axis=1)


def _swa_kernel(sink_ref, h_ref, hprev_ref, cos_ref, sin_ref, cosp_ref, sinp_ref,
                wq_ref, wkv_ref, wo_ref, ln_ref, wup_ref, wdn_ref, out_ref, o_scr):
    h = h_ref[0]
    tm = h.shape[0]
    hb = h.astype(BF16)
    hcat = jnp.concatenate([hprev_ref[0].astype(BF16), hb], axis=0)
    cos_cat = jnp.concatenate([cosp_ref[...], cos_ref[...]], axis=0)
    sin_cat = jnp.concatenate([sinp_ref[...], sin_ref[...]], axis=0)

    q = _rope(_dot(hb, wq_ref[...]), cos_ref[...], sin_ref[...]) * (HEAD_DIM ** -0.5)
    kv = _dot(hcat, wkv_ref[...])
    kdim = 2 * N_HEADS_KV * HEAD_DIM
    kr = _rope(kv[:, :kdim], cos_cat, sin_cat).astype(BF16)
    vb = kv[:, kdim:].astype(BF16)
    qb = q.astype(BF16)

    nq = GROUP * BLOCK
    first_row = pl.program_id(1) * tm
    qi = lax.broadcasted_iota(jnp.int32, (nq, 2 * BLOCK), 0) & (BLOCK - 1)
    kj = lax.broadcasted_iota(jnp.int32, (nq, 2 * BLOCK), 1)
    rel = BLOCK + qi - kj
    in_window = (rel >= 0) & (rel < WINDOW)
    head_of_row = lax.broadcasted_iota(jnp.int32, (nq, 1), 0) // BLOCK
    first = lax.broadcasted_iota(jnp.int32, (BLOCK, LANES), 1) < HEAD_DIM
    zero = jnp.zeros((BLOCK, LANES), BF16)

    units = [(j, g) for j in range(tm // BLOCK) for g in range(N_HEADS_KV)]
    scores = []
    for j, g in units:
        rows = slice(j * BLOCK, (j + 1) * BLOCK)
        qa = qb[rows, 2 * g * LANES:(2 * g + 1) * LANES]
        qc = qb[rows, (2 * g + 1) * LANES:(2 * g + 2) * LANES]
        qs = jnp.concatenate([jnp.where(first, qa, zero), jnp.where(first, zero, qa),
                              jnp.where(first, qc, zero), jnp.where(first, zero, qc)], axis=0)
        scores.append(_dot_nt(qs, kr[j * BLOCK:(j + 2) * BLOCK, g * LANES:(g + 1) * LANES]))
    probs, denoms = [], []
    for (j, g), s in zip(units, scores):
        key_row = first_row + (j - 1) * BLOCK + kj
        s = jnp.where(in_window & (key_row >= PAD_FRONT), s, -jnp.inf)
        sink = jnp.full((nq, 1), sink_ref[GROUP * g], F32)
        for i in range(1, GROUP):
            sink = jnp.where(head_of_row == i, sink_ref[GROUP * g + i], sink)
        m = jnp.maximum(jnp.max(s, axis=-1, keepdims=True), sink)
        p = jnp.exp(s - m)
        denoms.append(jnp.sum(p, axis=-1, keepdims=True) + jnp.exp(sink - m))
        probs.append(p.astype(BF16))
    for (j, g), p, denom in zip(units, probs, denoms):
        o4 = _dot(p, vb[j * BLOCK:(j + 2) * BLOCK, g * LANES:(g + 1) * LANES]) / denom
        rows = slice(j * BLOCK, (j + 1) * BLOCK)
        o_scr[rows, 2 * g * LANES:(2 * g + 1) * LANES] = jnp.where(
            first, o4[:BLOCK], o4[BLOCK:2 * BLOCK]).astype(BF16)
        o_scr[rows, (2 * g + 1) * LANES:(2 * g + 2) * LANES] = jnp.where(
            first, o4[2 * BLOCK:3 * BLOCK], o4[3 * BLOCK:]).astype(BF16)

    mix = _dot(o_scr[...], wo_ref[...])
    out_ref[0] = _deepnorm_mlp(h, mix, ln_ref, wup_ref, wdn_ref)


def _swa_layer(h, sinks, cos, sin_signed, wq, wkv, wo, ln, wup, wdn):
    bsz, lp, d = h.shape
    tm = TM_ATT
    nblk = tm // BLOCK
    const = lambda shape: pl.BlockSpec(shape, lambda bi, i: (0, 0), pipeline_mode=pl.Buffered(1))
    prev = lambda bi, i: (bi, jnp.maximum(i * nblk - 1, 0), 0)
    prev2 = lambda bi, i: (jnp.maximum(i * nblk - 1, 0), 0)
    return pl.pallas_call(
        _swa_kernel,
        grid=(bsz, lp // tm),
        in_specs=[
            pl.BlockSpec(memory_space=pltpu.SMEM),
            pl.BlockSpec((1, tm, d), lambda bi, i: (bi, i, 0)),
            pl.BlockSpec((1, BLOCK, d), prev),
            pl.BlockSpec((tm, LANES), lambda bi, i: (i, 0)),
            pl.BlockSpec((tm, LANES), lambda bi, i: (i, 0)),
            pl.BlockSpec((BLOCK, LANES), prev2),
            pl.BlockSpec((BLOCK, LANES), prev2),
            const(wq.shape), const(wkv.shape), const(wo.shape), const(ln.shape),
            const(wup.shape), const(wdn.shape),
        ],
        out_specs=pl.BlockSpec((1, tm, d), lambda bi, i: (bi, i, 0)),
        out_shape=jax.ShapeDtypeStruct((bsz, lp, d), F32),
        scratch_shapes=[pltpu.VMEM((tm, d), BF16)],
        compiler_params=pltpu.CompilerParams(
            dimension_semantics=("parallel", "parallel"), vmem_limit_bytes=VMEM_LIMIT),
        name="swa_layer",
    )(sinks, h, h, cos, sin_signed, cos, sin_signed, wq, wkv, wo, ln, wup, wdn)


def kernel(x, meta_tokens, a_mu, a_w_r, a_w_k, a_w_v, a_w_o, a_w0, a_w1, a_w2, a_a0, a_a1, a_a2,
           a_g1, a_g2, a_k_k, a_k_a, a_r_k, a_gn_w, a_gn_b, kv_w_k, kv_w_v, b_w_q, b_sinks, b_w_o,
           mlp_w_up, mlp_w_down, ln_g, ln_b):
    bsz, seq, d = x.shape
    lp = PAD_FRONT + N_META + seq
    bf = lambda t: t.astype(BF16)

    meta = jnp.broadcast_to(meta_tokens[None].astype(x.dtype), (bsz, N_META, d))
    h0 = jnp.concatenate([jnp.zeros((bsz, PAD_FRONT, d), x.dtype), meta, x], axis=1)
    h0_flat = h0.reshape(bsz * lp, d)

    head_of_lane = jnp.arange(d) // HEAD_DIM
    gsum = (head_of_lane[:, None] == jnp.arange(LANES)[None, :]).astype(BF16)
    gexp = jnp.concatenate([gsum.T, gsum.T], axis=0)

    vec = jnp.stack([a_w0[0], a_a0[0], a_k_k[0], a_k_a[0], a_r_k[0].reshape(d)])
    r, lw, k, v, kk, b, g, bonus = _rwkv_proj(
        h0_flat, lp, a_mu[0], vec, bf(a_w_r[0]), bf(a_w_k[0]), bf(a_w_v[0]),
        bf(a_w1[0]), bf(a_w2[0]), bf(a_a1[0]), bf(a_a2[0]), bf(a_g1[0]), bf(a_g2[0]), gsum, gexp)

    shp = (bsz, lp, d)
    rw, y0, mlr, sadd, dec = _rwkv_chunk(*(t.reshape(shp) for t in (r, lw, k, v, kk, b)))
    hv = jnp.stack([a_gn_w[0], a_gn_b[0]])
    o = _rwkv_state(rw, y0, mlr, sadd, dec, bonus.reshape(shp), g.reshape(shp), hv, gsum, gexp)

    ln0 = jnp.stack([ln_g[0, 0], ln_b[0, 0], ln_g[0, 1], ln_b[0, 1]])
    h1 = _post_mlp(o.reshape(bsz * lp, d), h0_flat, bf(a_w_o[0]), ln0,
                   bf(mlp_w_up[0]), bf(mlp_w_down[0]))

    pos = (jnp.arange(lp) - PAD_FRONT).astype(F32)
    inv_freq = 1.0 / (ROPE_THETA ** (jnp.arange(0, HEAD_DIM, 2, dtype=F32) / HEAD_DIM))
    ang = pos[:, None] * inv_freq[None, :]
    cos = jnp.tile(jnp.cos(ang), (1, 4))
    sin_signed = jnp.tile(jnp.concatenate([-jnp.sin(ang), jnp.sin(ang)], axis=1), (1, 2))

    ln1 = jnp.stack([ln_g[1, 0], ln_b[1, 0], ln_g[1, 1], ln_b[1, 1]])
    twice = lambda w: jnp.tile(w.reshape(d, N_HEADS_KV, 1, HEAD_DIM), (1, 1, 2, 1)).reshape(d, -1)
    wkv = bf(jnp.concatenate([twice(kv_w_k), twice(kv_w_v)], axis=1))
    h2 = _swa_layer(h1.reshape(shp), b_sinks[0], cos, sin_signed, bf(b_w_q[0]), wkv, bf(b_w_o[0]),
                    ln1, bf(mlp_w_up[1]), bf(mlp_w_down[1]))
    return h2[:, PAD_FRONT + N_META:]
```

```python
import functools

import jax
import jax.numpy as jnp
from jax import lax
from jax.experimental import pallas as pl
from jax.experimental.pallas import tpu as pltpu

D_MODEL = 1024
N_META = 16
HEAD_DIM = 64
N_HEADS = D_MODEL // HEAD_DIM
N_HEADS_KV = 4
GROUP = N_HEADS // N_HEADS_KV
GN_EPS = 64e-5
WINDOW = 128
BLOCK = 128
PAD_FRONT = BLOCK - N_META
ROPE_THETA = 10000.0
D_FF = 4 * D_MODEL
DEPTH = 2
ALPHA = (2.0 * DEPTH) ** 0.25
LN_EPS = 1e-5

LANES = 128
N_PAIRS = D_MODEL // LANES
CHUNK = 64
TM_ROWS = 512
TA_ROWS = 384
TM_ATT = 512
FF_CHUNK = 1024
VMEM_LIMIT = 56 * 1024 * 1024

BF16 = jnp.bfloat16
F32 = jnp.float32


def _dot(a, b):
    return jnp.dot(a, b, preferred_element_type=F32)


def _dot_nt(a, b):
    return lax.dot_general(a, b, (((1,), (1,)), ((), ())), preferred_element_type=F32)


def _dot_tn(a, b):
    return lax.dot_general(a, b, (((0,), (0,)), ((), ())), preferred_element_type=F32)


def _split_bf16(x):
    hi = x.astype(BF16)
    lo = (x - hi.astype(F32)).astype(BF16)
    return jnp.concatenate([hi, lo], axis=1)


def _block_diag(x):
    first = lax.broadcasted_iota(jnp.int32, x.shape, 1) < HEAD_DIM
    zero = jnp.zeros_like(x)
    return jnp.concatenate([jnp.where(first, x, zero), jnp.where(first, zero, x)], axis=0)


def _diag_blocks(full):
    first = lax.broadcasted_iota(jnp.int32, (HEAD_DIM, LANES), 1) < HEAD_DIM
    return jnp.where(first, full[:HEAD_DIM], full[HEAD_DIM:])


def _layer_norm(x, g, b):
    mu = jnp.mean(x, axis=-1, keepdims=True)
    xc = x - mu
    var = jnp.mean(xc * xc, axis=-1, keepdims=True)
    return xc * lax.rsqrt(var + LN_EPS) * g + b


def _deepnorm_mlp(h, mix, ln_ref, wup_ref, wdn_ref):
    h1 = _layer_norm(ALPHA * h + mix, ln_ref[0:1, :], ln_ref[1:2, :])
    hb = h1.astype(BF16)
    acc = jnp.zeros_like(h1)
    for c in range(D_FF // FF_CHUNK):
        up = _dot(hb, wup_ref[:, c * FF_CHUNK:(c + 1) * FF_CHUNK])
        up = jnp.maximum(up, 0.0)
        acc = acc + _dot((up * up).astype(BF16), wdn_ref[c * FF_CHUNK:(c + 1) * FF_CHUNK, :])
    return _layer_norm(ALPHA * h1 + acc, ln_ref[2:3, :], ln_ref[3:4, :])


def _proj_kernel(seq_len, h_ref, hprev_ref, mu_ref, vec_ref, wr_ref, wk_ref, wv_ref,
                 w1_ref, w2_ref, a1_ref, a2_ref, g1_ref, g2_ref, gsum_ref, gexp_ref,
                 r_out, lw_out, k_out, v_out, kk_out, b_out, g_out, bonus_out):
    x = h_ref[...]
    tm = x.shape[0]
    row = lax.broadcasted_iota(jnp.int32, x.shape, 0)
    tile_start = pl.program_id(0) * tm
    seq_start = (tile_start + seq_len - 1) // seq_len * seq_len - tile_start
    prev_row = hprev_ref[7:8, :]
    x_prev = jnp.where(row == 0, prev_row, pltpu.roll(x, 1, axis=0))
    x_prev = jnp.where(row == seq_start, 0.0, x_prev)
    xx = x_prev - x

    def mix(i):
        return (x + xx * mu_ref[i:i + 1, :]).astype(BF16)

    r = _dot(mix(0), wr_ref[...])
    k = _dot(mix(2), wk_ref[...])
    v = _dot(mix(3), wv_ref[...])
    w_lora = _dot(jnp.tanh(_dot(mix(1), w1_ref[...])).astype(BF16), w2_ref[...])
    a_lora = _dot(_dot(mix(4), a1_ref[...]).astype(BF16), a2_ref[...])
    g = _dot(jax.nn.sigmoid(_dot(mix(5), g1_ref[...])).astype(BF16), g2_ref[...])

    w0 = vec_ref[0:1, :]
    a0 = vec_ref[1:2, :]
    k_k = vec_ref[2:3, :]
    k_a = vec_ref[3:4, :]
    r_k = vec_ref[4:5, :]
    w = -jax.nn.softplus(-(w0 + w_lora)) - 0.5
    lw = -jnp.exp(w)
    a = jax.nn.sigmoid(a0 + a_lora)

    kk = k * k_k
    ss = _dot((kk * kk).astype(BF16), gsum_ref[...])
    inv = lax.rsqrt(jnp.maximum(ss, 1e-24))
    kk = kk * _dot(_split_bf16(inv), gexp_ref[...])
    k2 = k * (1.0 + (a - 1.0) * k_a)
    rk_sum = _dot((r * k2 * r_k).astype(BF16), gsum_ref[...])

    lw_out[...] = lw
    r_out[...] = r.astype(r_out.dtype)
    k_out[...] = k2.astype(k_out.dtype)
    v_out[...] = v.astype(v_out.dtype)
    kk_out[...] = kk.astype(kk_out.dtype)
    b_out[...] = (kk * a).astype(b_out.dtype)
    g_out[...] = g.astype(g_out.dtype)
    bonus_out[...] = (_dot(_split_bf16(rk_sum), gexp_ref[...]) * v).astype(bonus_out.dtype)


def _rwkv_proj(h_flat, seq_len, mu, vec, wr, wk, wv, w1, w2, a1, a2, g1, g2, gsum, gexp):
    rows = h_flat.shape[0]
    tm = TM_ROWS
    const = lambda shape: pl.BlockSpec(shape, lambda i: (0, 0), pipeline_mode=pl.Buffered(1))
    tile = pl.BlockSpec((tm, D_MODEL), lambda i: (i, 0))
    f32_sds = jax.ShapeDtypeStruct((rows, D_MODEL), F32)
    bf16_sds = jax.ShapeDtypeStruct((rows, D_MODEL), BF16)
    return pl.pallas_call(
        functools.partial(_proj_kernel, seq_len),
        grid=(rows // tm,),
        in_specs=[
            tile,
            pl.BlockSpec((8, D_MODEL), lambda i: (jnp.maximum(i * (tm // 8) - 1, 0), 0)),
            const(mu.shape), const(vec.shape), const(wr.shape), const(wk.shape), const(wv.shape),
            const(w1.shape), const(w2.shape), const(a1.shape), const(a2.shape),
            const(g1.shape), const(g2.shape), const(gsum.shape), const(gexp.shape),
        ],
        out_specs=[tile] * 8,
        out_shape=[bf16_sds, f32_sds] + [bf16_sds] * 6,
        compiler_params=pltpu.CompilerParams(
            dimension_semantics=("parallel",), vmem_limit_bytes=VMEM_LIMIT),
        name="rwkv_proj",
    )(h_flat, h_flat, mu, vec, wr, wk, wv, w1, w2, a1, a2, g1, g2, gsum, gexp)


def _chunk_kernel(r_ref, lw_ref, k_ref, v_ref, kk_ref, b_ref,
                  rw_out, y0_out, mlr_out, sadd_out, dec_out,
                  at_s, rt_s, kh_s, bh_s, kb_s, bb_s):
    C = CHUNK
    ta = lw_ref.shape[1]
    nch = ta // C

    lw = lw_ref[0]
    row_in_chunk = lax.broadcasted_iota(jnp.int32, lw.shape, 0) & (C - 1)
    c = lw
    for s in (1, 2, 4, 8, 16, 32):
        c = c + jnp.where(row_in_chunk >= s, pltpu.roll(c, s, axis=0), 0.0)
    c3 = c.reshape(nch, C, D_MODEL)
    c_last = c3[:, C - 1:C, :]
    e_cc = jnp.exp(c_last - c3).reshape(ta, D_MODEL)
    dec_out[0] = jnp.exp(c_last)
    e_nc = jnp.exp(-c)
    k = k_ref[0].astype(F32)
    b = b_ref[0].astype(F32)
    at_s[...] = (-kk_ref[0].astype(F32) * jnp.exp(c - lw)).astype(BF16)
    rt_s[...] = (r_ref[0].astype(F32) * jnp.exp(c)).astype(BF16)
    kh_s[...] = (k * e_nc).astype(BF16)
    bh_s[...] = (b * e_nc).astype(BF16)
    kb_s[...] = (k * e_cc).astype(BF16)
    bb_s[...] = (b * e_cc).astype(BF16)

    t_idx = lax.broadcasted_iota(jnp.int32, (C, LANES), 0)
    s_idx = lax.broadcasted_iota(jnp.int32, (C, LANES), 1) & (HEAD_DIM - 1)
    strict = s_idx < t_idx
    incl = s_idx <= t_idx
    eye = (s_idx == t_idx).astype(F32)

    def chunk_body(j, carry):
        rows = pl.ds(pl.multiple_of(j * C, C), C)
        pairs = range(N_PAIRS)
        lanes = [slice(p * LANES, (p + 1) * LANES) for p in pairs]
        at = [at_s[rows, ls] for ls in lanes]
        rt = [rt_s[rows, ls] for ls in lanes]
        ar = [jnp.concatenate([at[p], rt[p]], axis=0) for p in pairs]
        gk = [_dot_nt(ar[p], _block_diag(kh_s[rows, lanes[p]])) for p in pairs]
        gb = [_dot_nt(ar[p], _block_diag(bh_s[rows, lanes[p]])) for p in pairs]
        a_ak = [jnp.where(strict, gk[p][:C], 0.0).astype(BF16) for p in pairs]
        a_rk = [jnp.where(incl, gk[p][C:], 0.0).astype(BF16) for p in pairs]
        a_rb = [jnp.where(incl, gb[p][C:], 0.0).astype(BF16) for p in pairs]
        a_ab = [jnp.where(strict, gb[p][:C], 0.0) for p in pairs]
        tinv = [eye + a_ab[p] for p in pairs]
        npow = [a_ab[p].astype(BF16) for p in pairs]
        npow = [_dot(npow[p], _block_diag(npow[p])).astype(BF16) for p in pairs]
        for _ in range(4):
            res = [_dot(jnp.concatenate([npow[p], tinv[p].astype(BF16)], axis=0),
                        _block_diag(npow[p])) for p in pairs]
            tinv = [tinv[p] + res[p][C:] for p in pairs]
            npow = [res[p][:C].astype(BF16) for p in pairs]
        tinv = [(tinv[p] + _dot(tinv[p].astype(BF16), _block_diag(npow[p]))).astype(BF16)
                for p in pairs]
        vv = [v_ref[0, rows, ls] for ls in lanes]
        x = [_dot(a_ak[p], _block_diag(vv[p])).astype(BF16) for p in pairs]
        wu = [_dot(tinv[p], jnp.concatenate([_block_diag(at[p]), _block_diag(x[p])], axis=1))
              for p in pairs]
        w_b = [wu[p][:, :LANES].astype(BF16) for p in pairs]
        u0_b = [wu[p][:, LANES:].astype(BF16) for p in pairs]
        for p in pairs:
            rw = rt[p].astype(F32) + _dot(a_rb[p], _block_diag(w_b[p]))
            rw_out[0, rows, lanes[p]] = rw.astype(rw_out.dtype)
        for p in pairs:
            y0_out[0, rows, lanes[p]] = _dot(
                jnp.concatenate([a_rk[p], a_rb[p]], axis=1),
                jnp.concatenate([_block_diag(vv[p]), _block_diag(u0_b[p])], axis=0))
        for p in pairs:
            bb = bb_s[rows, lanes[p]]
            mlr_out[0, j, :, lanes[p]] = _diag_blocks(_dot_tn(w_b[p], bb)).astype(mlr_out.dtype)
            sadd_out[0, j, :, lanes[p]] = _diag_blocks(
                _dot_tn(jnp.concatenate([vv[p], u0_b[p]], axis=0),
                        jnp.concatenate([kb_s[rows, lanes[p]], bb], axis=0)))
        return carry

    lax.fori_loop(0, nch, chunk_body, 0)


def _rwkv_chunk(r, lw, k, v, kk, b):
    bsz, lp, d = r.shape
    ta = TA_ROWS
    nch = ta // CHUNK
    nc = lp // CHUNK
    tile = pl.BlockSpec((1, ta, d), lambda bi, i: (bi, i, 0))
    per_chunk = lambda rows: pl.BlockSpec((1, nch, rows, d), lambda bi, i: (bi, i, 0, 0))
    return pl.pallas_call(
        _chunk_kernel,
        grid=(bsz, lp // ta),
        in_specs=[tile] * 6,
        out_specs=[tile, tile, per_chunk(HEAD_DIM), per_chunk(HEAD_DIM), per_chunk(1)],
        out_shape=[
            jax.ShapeDtypeStruct((bsz, lp, d), BF16),
            jax.ShapeDtypeStruct((bsz, lp, d), F32),
            jax.ShapeDtypeStruct((bsz, nc, HEAD_DIM, d), BF16),
            jax.ShapeDtypeStruct((bsz, nc, HEAD_DIM, d), F32),
            jax.ShapeDtypeStruct((bsz, nc, 1, d), F32),
        ],
        scratch_shapes=[pltpu.VMEM((ta, d), BF16)] * 6,
        compiler_params=pltpu.CompilerParams(
            dimension_semantics=("parallel", "parallel"), vmem_limit_bytes=VMEM_LIMIT),
        name="rwkv_chunk",
    )(r, lw, k, v, kk, b)


def _state_kernel(rw_ref, y0_ref, mlr_ref, sadd_ref, dec_ref, bonus_ref, g_ref, hv_ref,
                  gsum_ref, gexp_ref, o_ref, s_ref, y_s):
    bsz = rw_ref.shape[0]

    @pl.when(pl.program_id(0) == 0)
    def _():
        s_ref[...] = jnp.zeros_like(s_ref)

    def seq_body(bi, carry):
        for p in range(N_PAIRS):
            ls = slice(p * LANES, (p + 1) * LANES)
            s0 = s_ref[bi, :, ls]
            s0b = s0.astype(BF16)
            y_s[bi, :, ls] = y0_ref[bi, :, ls] + _dot_nt(rw_ref[bi, :, ls], _block_diag(s0b))
            s_ref[bi, :, ls] = (s0 * dec_ref[bi, 0, :, ls]
                                + _dot(s0b, _block_diag(mlr_ref[bi, 0, :, ls]))
                                + sadd_ref[bi, 0, :, ls])
        return carry

    lax.fori_loop(0, bsz, seq_body, 0)

    y = y_s[...].reshape(bsz * CHUNK, D_MODEL)
    inv_n = 1.0 / HEAD_DIM
    mean = _dot(_split_bf16(_dot(y.astype(BF16), gsum_ref[...]) * inv_n), gexp_ref[...])
    yc = y - mean
    var = _dot((yc * yc).astype(BF16), gsum_ref[...]) * inv_n
    rstd = _dot(_split_bf16(lax.rsqrt(var + GN_EPS)), gexp_ref[...])
    yn = yc * rstd * hv_ref[0:1, :] + hv_ref[1:2, :]
    bonus = bonus_ref[...].reshape(bsz * CHUNK, D_MODEL).astype(F32)
    g = g_ref[...].reshape(bsz * CHUNK, D_MODEL).astype(F32)
    o_ref[...] = ((yn + bonus) * g).astype(o_ref.dtype).reshape(o_ref.shape)


def _rwkv_state(rw, y0, mlr, sadd, dec, bonus, g, hv, gsum, gexp):
    bsz, lp, d = rw.shape
    nc = lp // CHUNK
    tile = pl.BlockSpec((bsz, CHUNK, d), lambda n: (0, n, 0))
    per_chunk = lambda rows: pl.BlockSpec((bsz, 1, rows, d), lambda n: (0, n, 0, 0))
    const = lambda shape: pl.BlockSpec(shape, lambda n: (0, 0))
    return pl.pallas_call(
        _state_kernel,
        grid=(nc,),
        in_specs=[tile, tile, per_chunk(HEAD_DIM), per_chunk(HEAD_DIM), per_chunk(1), tile, tile,
                  const(hv.shape), const(gsum.shape), const(gexp.shape)],
        out_specs=tile,
        out_shape=jax.ShapeDtypeStruct((bsz, lp, d), BF16),
        scratch_shapes=[pltpu.VMEM((bsz, HEAD_DIM, d), F32), pltpu.VMEM((bsz, CHUNK, d), F32)],
        compiler_params=pltpu.CompilerParams(
            dimension_semantics=("arbitrary",), vmem_limit_bytes=VMEM_LIMIT),
        name="rwkv_state",
    )(rw, y0, mlr, sadd, dec, bonus, g, hv, gsum, gexp)


def _post_mlp_kernel(o_ref, h_ref, wo_ref, ln_ref, wup_ref, wdn_ref, out_ref):
    mix = _dot(o_ref[...], wo_ref[...])
    out_ref[...] = _deepnorm_mlp(h_ref[...], mix, ln_ref, wup_ref, wdn_ref)


def _post_mlp(o_flat, h_flat, wo, ln, wup, wdn):
    rows = h_flat.shape[0]
    tm = TM_ROWS
    const = lambda shape: pl.BlockSpec(shape, lambda i: (0, 0), pipeline_mode=pl.Buffered(1))
    tile = pl.BlockSpec((tm, D_MODEL), lambda i: (i, 0))
    return pl.pallas_call(
        _post_mlp_kernel,
        grid=(rows // tm,),
        in_specs=[tile, tile, const(wo.shape), const(ln.shape), const(wup.shape), const(wdn.shape)],
        out_specs=tile,
        out_shape=jax.ShapeDtypeStruct((rows, D_MODEL), F32),
        compiler_params=pltpu.CompilerParams(
            dimension_semantics=("parallel",), vmem_limit_bytes=VMEM_LIMIT),
        name="post_mlp",
    )(o_flat, h_flat, wo, ln, wup, wdn)


def _rope(t, cos, sin_signed):
    out = []
    lane = lax.broadcasted_iota(jnp.int32, (t.shape[0], LANES), 1)
    first_half = (lane % HEAD_DIM) < (HEAD_DIM // 2)
    for j in range(t.shape[1] // LANES):
        tj = t[:, j * LANES:(j + 1) * LANES]
        rot = jnp.where(first_half, pltpu.roll(tj, LANES - HEAD_DIM // 2, axis=1),
                        pltpu.roll(tj, HEAD_DIM // 2, axis=1))
        out.append(tj * cos + rot * sin_signed)
    return jnp.concatenate(out, axis=1)


def _swa_kernel(sink_ref, h_ref, hprev_ref, cos_ref, sin_ref, cosp_ref, sinp_ref,
                wq_ref, wkv_ref, wo_ref, ln_ref, wup_ref, wdn_ref, out_ref, o_scr):
    h = h_ref[0]
    tm = h.shape[0]
    hb = h.astype(BF16)
    hcat = jnp.concatenate([hprev_ref[0].astype(BF16), hb], axis=0)
    cos_cat = jnp.concatenate([cosp_ref[...], cos_ref[...]], axis=0)
    sin_cat = jnp.concatenate([sinp_ref[...], sin_ref[...]], axis=0)

    q = _rope(_dot(hb, wq_ref[...]), cos_ref[...], sin_ref[...]) * (HEAD_DIM ** -0.5)
    kv = _dot(hcat, wkv_ref[...])
    kdim = 2 * N_HEADS_KV * HEAD_DIM
    kr = _rope(kv[:, :kdim], cos_cat, sin_cat).astype(BF16)
    vb = kv[:, kdim:].astype(BF16)
    qb = q.astype(BF16)

    nq = GROUP * BLOCK
    first_row = BLOCK + pl.program_id(1) * tm
    qi = lax.broadcasted_iota(jnp.int32, (nq, 2 * BLOCK), 0) & (BLOCK - 1)
    kj = lax.broadcasted_iota(jnp.int32, (nq, 2 * BLOCK), 1)
    rel = BLOCK + qi - kj
    in_window = (rel >= 0) & (rel < WINDOW)
    head_of_row = lax.broadcasted_iota(jnp.int32, (nq, 1), 0) // BLOCK
    first = lax.broadcasted_iota(jnp.int32, (BLOCK, LANES), 1) < HEAD_DIM
    zero = jnp.zeros((BLOCK, LANES), BF16)

    groups = range(N_HEADS_KV)
    for j in range(tm // BLOCK):
        rows = slice(j * BLOCK, (j + 1) * BLOCK)
        keys = slice(j * BLOCK, (j + 2) * BLOCK)
        key_row = first_row + (j - 1) * BLOCK + kj
        mask = in_window & (key_row >= PAD_FRONT)
        scores = []
        for g in groups:
            qa = qb[rows, 2 * g * LANES:(2 * g + 1) * LANES]
            qc = qb[rows, (2 * g + 1) * LANES:(2 * g + 2) * LANES]
            qs = jnp.concatenate([jnp.where(first, qa, zero), jnp.where(first, zero, qa),
                                  jnp.where(first, qc, zero), jnp.where(first, zero, qc)], axis=0)
            scores.append(_dot_nt(qs, kr[keys, g * LANES:(g + 1) * LANES]))
        probs, denoms = [], []
        for g in groups:
            s = jnp.where(mask, scores[g], -jnp.inf)
            sink = jnp.full((nq, 1), sink_ref[GROUP * g], F32)
            for i in range(1, GROUP):
                sink = jnp.where(head_of_row == i, sink_ref[GROUP * g + i], sink)
            m = jnp.maximum(jnp.max(s, axis=-1, keepdims=True), sink)
            p = jnp.exp(s - m)
            denoms.append(jnp.sum(p, axis=-1, keepdims=True) + jnp.exp(sink - m))
            probs.append(p.astype(BF16))
        for g in groups:
            o4 = _dot(probs[g], vb[keys, g * LANES:(g + 1) * LANES]) / denoms[g]
            o_scr[rows, 2 * g * LANES:(2 * g + 1) * LANES] = jnp.where(
                first, o4[:BLOCK], o4[BLOCK:2 * BLOCK]).astype(BF16)
            o_scr[rows, (2 * g + 1) * LANES:(2 * g + 2) * LANES] = jnp.where(
                first, o4[2 * BLOCK:3 * BLOCK], o4[3 * BLOCK:]).astype(BF16)

    mix = _dot(o_scr[...], wo_ref[...])
    out_ref[0] = _deepnorm_mlp(h, mix, ln_ref, wup_ref, wdn_ref)


def _swa_layer(h, sinks, cos, sin_signed, wq, wkv, wo, ln, wup, wdn):
    bsz, lp, d = h.shape
    tm = TM_ATT
    nblk = tm // BLOCK
    seq = lp - BLOCK
    const = lambda shape: pl.BlockSpec(shape, lambda bi, i: (0, 0), pipeline_mode=pl.Buffered(1))
    h_tile = pl.BlockSpec((pl.Element(1), pl.Element(tm), pl.Element(d)),
                          lambda bi, i: (bi, pl.multiple_of(BLOCK + i * tm, BLOCK), 0))
    table_tile = pl.BlockSpec((pl.Element(tm), pl.Element(LANES)),
                              lambda bi, i: (pl.multiple_of(BLOCK + i * tm, BLOCK), 0))
    return pl.pallas_call(
        _swa_kernel,
        grid=(bsz, seq // tm),
        in_specs=[
            pl.BlockSpec(memory_space=pltpu.SMEM),
            h_tile,
            pl.BlockSpec((1, BLOCK, d), lambda bi, i: (bi, i * nblk, 0)),
            table_tile,
            table_tile,
            pl.BlockSpec((BLOCK, LANES), lambda bi, i: (i * nblk, 0)),
            pl.BlockSpec((BLOCK, LANES), lambda bi, i: (i * nblk, 0)),
            const(wq.shape), const(wkv.shape), const(wo.shape), const(ln.shape),
            const(wup.shape), const(wdn.shape),
        ],
        out_specs=pl.BlockSpec((1, tm, d), lambda bi, i: (bi, i, 0)),
        out_shape=jax.ShapeDtypeStruct((bsz, seq, d), F32),
        scratch_shapes=[pltpu.VMEM((tm, d), BF16)],
        compiler_params=pltpu.CompilerParams(
            dimension_semantics=("parallel", "parallel"), vmem_limit_bytes=VMEM_LIMIT),
        name="swa_layer",
    )(sinks, h, h, cos, sin_signed, cos, sin_signed, wq, wkv, wo, ln, wup, wdn)


def kernel(x, meta_tokens, a_mu, a_w_r, a_w_k, a_w_v, a_w_o, a_w0, a_w1, a_w2, a_a0, a_a1, a_a2,
           a_g1, a_g2, a_k_k, a_k_a, a_r_k, a_gn_w, a_gn_b, kv_w_k, kv_w_v, b_w_q, b_sinks, b_w_o,
           mlp_w_up, mlp_w_down, ln_g, ln_b):
    bsz, seq, d = x.shape
    lp = PAD_FRONT + N_META + seq
    bf = lambda t: t.astype(BF16)

    meta = jnp.broadcast_to(meta_tokens[None].astype(x.dtype), (bsz, N_META, d))
    h0 = jnp.concatenate([jnp.zeros((bsz, PAD_FRONT, d), x.dtype), meta, x], axis=1)
    h0_flat = h0.reshape(bsz * lp, d)

    head_of_lane = jnp.arange(d) // HEAD_DIM
    gsum = (head_of_lane[:, None] == jnp.arange(LANES)[None, :]).astype(BF16)
    gexp = jnp.concatenate([gsum.T, gsum.T], axis=0)

    vec = jnp.stack([a_w0[0], a_a0[0], a_k_k[0], a_k_a[0], a_r_k[0].reshape(d)])
    r, lw, k, v, kk, b, g, bonus = _rwkv_proj(
        h0_flat, lp, a_mu[0], vec, bf(a_w_r[0]), bf(a_w_k[0]), bf(a_w_v[0]),
        bf(a_w1[0]), bf(a_w2[0]), bf(a_a1[0]), bf(a_a2[0]), bf(a_g1[0]), bf(a_g2[0]), gsum, gexp)

    shp = (bsz, lp, d)
    rw, y0, mlr, sadd, dec = _rwkv_chunk(*(t.reshape(shp) for t in (r, lw, k, v, kk, b)))
    hv = jnp.stack([a_gn_w[0], a_gn_b[0]])
    o = _rwkv_state(rw, y0, mlr, sadd, dec, bonus.reshape(shp), g.reshape(shp), hv, gsum, gexp)

    ln0 = jnp.stack([ln_g[0, 0], ln_b[0, 0], ln_g[0, 1], ln_b[0, 1]])
    h1 = _post_mlp(o.reshape(bsz * lp, d), h0_flat, bf(a_w_o[0]), ln0,
                   bf(mlp_w_up[0]), bf(mlp_w_down[0]))

    pos = (jnp.arange(lp) - PAD_FRONT).astype(F32)
    inv_freq = 1.0 / (ROPE_THETA ** (jnp.arange(0, HEAD_DIM, 2, dtype=F32) / HEAD_DIM))
    ang = pos[:, None] * inv_freq[None, :]
    cos = jnp.tile(jnp.cos(ang), (1, 4))
    sin_signed = jnp.tile(jnp.concatenate([-jnp.sin(ang), jnp.sin(ang)], axis=1), (1, 2))

    ln1 = jnp.stack([ln_g[1, 0], ln_b[1, 0], ln_g[1, 1], ln_b[1, 1]])
    twice = lambda w: jnp.tile(w.reshape(d, N_HEADS_KV, 1, HEAD_DIM), (1, 1, 2, 1)).reshape(d, -1)
    wkv = bf(jnp.concatenate([twice(kv_w_k), twice(kv_w_v)], axis=1))
    return _swa_layer(h1.reshape(shp), b_sinks[0], cos, sin_signed, bf(b_w_q[0]), wkv, bf(b_w_o[0]),
                      ln1, bf(mlp_w_up[1]), bf(mlp_w_down[1]))
```

```python
import functools

import jax
import jax.numpy as jnp
from jax import lax
from jax.experimental import pallas as pl
from jax.experimental.pallas import tpu as pltpu

D_MODEL = 1024
N_META = 16
HEAD_DIM = 64
N_HEADS = D_MODEL // HEAD_DIM
N_HEADS_KV = 4
GROUP = N_HEADS // N_HEADS_KV
GN_EPS = 64e-5
WINDOW = 128
BLOCK = 128
PAD_FRONT = BLOCK - N_META
ROPE_THETA = 10000.0
D_FF = 4 * D_MODEL
DEPTH = 2
ALPHA = (2.0 * DEPTH) ** 0.25
LN_EPS = 1e-5

LANES = 128
N_PAIRS = D_MODEL // LANES
CHUNK = 64
TM_ROWS = 512
TA_ROWS = 384
CHUNKS_PER_ITER = 2
SEQS_PER_ITER = 2
TM_ATT = 512
FF_CHUNK = 1024
VMEM_LIMIT = 56 * 1024 * 1024

BF16 = jnp.bfloat16
F32 = jnp.float32


def _dot(a, b):
    return jnp.dot(a, b, preferred_element_type=F32)


def _dot_nt(a, b):
    return lax.dot_general(a, b, (((1,), (1,)), ((), ())), preferred_element_type=F32)


def _dot_tn(a, b):
    return lax.dot_general(a, b, (((0,), (0,)), ((), ())), preferred_element_type=F32)


def _split_bf16(x):
    hi = x.astype(BF16)
    lo = (x - hi.astype(F32)).astype(BF16)
    return jnp.concatenate([hi, lo], axis=1)


def _block_diag(x):
    first = lax.broadcasted_iota(jnp.int32, x.shape, 1) < HEAD_DIM
    zero = jnp.zeros_like(x)
    return jnp.concatenate([jnp.where(first, x, zero), jnp.where(first, zero, x)], axis=0)


def _diag_blocks(full):
    first = lax.broadcasted_iota(jnp.int32, (HEAD_DIM, LANES), 1) < HEAD_DIM
    return jnp.where(first, full[:HEAD_DIM], full[HEAD_DIM:])


def _layer_norm(x, g, b):
    mu = jnp.mean(x, axis=-1, keepdims=True)
    xc = x - mu
    var = jnp.mean(xc * xc, axis=-1, keepdims=True)
    return xc * lax.rsqrt(var + LN_EPS) * g + b


def _deepnorm_mlp(h, mix, ln_ref, wup_ref, wdn_ref):
    h1 = _layer_norm(ALPHA * h + mix, ln_ref[0:1, :], ln_ref[1:2, :])
    hb = h1.astype(BF16)
    acc = jnp.zeros_like(h1)
    for c in range(D_FF // FF_CHUNK):
        up = _dot(hb, wup_ref[:, c * FF_CHUNK:(c + 1) * FF_CHUNK])
        up = jnp.maximum(up, 0.0)
        acc = acc + _dot((up * up).astype(BF16), wdn_ref[c * FF_CHUNK:(c + 1) * FF_CHUNK, :])
    return _layer_norm(ALPHA * h1 + acc, ln_ref[2:3, :], ln_ref[3:4, :])


def _proj_kernel(seq_len, h_ref, hprev_ref, mu_ref, vec_ref, wr_ref, wk_ref, wv_ref,
                 w1_ref, w2_ref, a1_ref, a2_ref, g1_ref, g2_ref, gsum_ref, gexp_ref,
                 r_out, lw_out, k_out, v_out, kk_out, b_out, g_out, bonus_out):
    x = h_ref[...]
    tm = x.shape[0]
    row = lax.broadcasted_iota(jnp.int32, x.shape, 0)
    tile_start = pl.program_id(0) * tm
    seq_start = (tile_start + seq_len - 1) // seq_len * seq_len - tile_start
    prev_row = hprev_ref[7:8, :]
    x_prev = jnp.where(row == 0, prev_row, pltpu.roll(x, 1, axis=0))
    x_prev = jnp.where(row == seq_start, 0.0, x_prev)
    xx = x_prev - x

    def mix(i):
        return (x + xx * mu_ref[i:i + 1, :]).astype(BF16)

    r = _dot(mix(0), wr_ref[...])
    k = _dot(mix(2), wk_ref[...])
    v = _dot(mix(3), wv_ref[...])
    w_lora = _dot(jnp.tanh(_dot(mix(1), w1_ref[...])).astype(BF16), w2_ref[...])
    a_lora = _dot(_dot(mix(4), a1_ref[...]).astype(BF16), a2_ref[...])
    g = _dot(jax.nn.sigmoid(_dot(mix(5), g1_ref[...])).astype(BF16), g2_ref[...])

    w0 = vec_ref[0:1, :]
    a0 = vec_ref[1:2, :]
    k_k = vec_ref[2:3, :]
    k_a = vec_ref[3:4, :]
    r_k = vec_ref[4:5, :]
    w = -jax.nn.softplus(-(w0 + w_lora)) - 0.5
    lw = -jnp.exp(w)
    a = jax.nn.sigmoid(a0 + a_lora)

    kk = k * k_k
    ss = _dot((kk * kk).astype(BF16), gsum_ref[...])
    inv = lax.rsqrt(jnp.maximum(ss, 1e-24))
    kk = kk * _dot(_split_bf16(inv), gexp_ref[...])
    k2 = k * (1.0 + (a - 1.0) * k_a)
    rk_sum = _dot((r * k2 * r_k).astype(BF16), gsum_ref[...])

    lw_out[...] = lw
    r_out[...] = r.astype(r_out.dtype)
    k_out[...] = k2.astype(k_out.dtype)
    v_out[...] = v.astype(v_out.dtype)
    kk_out[...] = kk.astype(kk_out.dtype)
    b_out[...] = (kk * a).astype(b_out.dtype)
    g_out[...] = g.astype(g_out.dtype)
    bonus_out[...] = (_dot(_split_bf16(rk_sum), gexp_ref[...]) * v).astype(bonus_out.dtype)


def _rwkv_proj(h_flat, seq_len, mu, vec, wr, wk, wv, w1, w2, a1, a2, g1, g2, gsum, gexp):
    rows = h_flat.shape[0]
    tm = TM_ROWS
    const = lambda shape: pl.BlockSpec(shape, lambda i: (0, 0), pipeline_mode=pl.Buffered(1))
    tile = pl.BlockSpec((tm, D_MODEL), lambda i: (i, 0))
    f32_sds = jax.ShapeDtypeStruct((rows, D_MODEL), F32)
    bf16_sds = jax.ShapeDtypeStruct((rows, D_MODEL), BF16)
    return pl.pallas_call(
        functools.partial(_proj_kernel, seq_len),
        grid=(rows // tm,),
        in_specs=[
            tile,
            pl.BlockSpec((8, D_MODEL), lambda i: (jnp.maximum(i * (tm // 8) - 1, 0), 0)),
            const(mu.shape), const(vec.shape), const(wr.shape), const(wk.shape), const(wv.shape),
            const(w1.shape), const(w2.shape), const(a1.shape), const(a2.shape),
            const(g1.shape), const(g2.shape), const(gsum.shape), const(gexp.shape),
        ],
        out_specs=[tile] * 8,
        out_shape=[bf16_sds, f32_sds] + [bf16_sds] * 6,
        compiler_params=pltpu.CompilerParams(
            dimension_semantics=("parallel",), vmem_limit_bytes=VMEM_LIMIT),
        name="rwkv_proj",
    )(h_flat, h_flat, mu, vec, wr, wk, wv, w1, w2, a1, a2, g1, g2, gsum, gexp)


def _chunk_kernel(r_ref, lw_ref, k_ref, v_ref, kk_ref, b_ref,
                  rw_out, y0_out, mlr_out, sadd_out, dec_out,
                  at_s, rt_s, kh_s, bh_s, kb_s, bb_s):
    C = CHUNK
    ta = lw_ref.shape[1]
    nch = ta // C

    lw = lw_ref[0]
    tri = (lax.broadcasted_iota(jnp.int32, (C, 2 * C), 0)
           >= (lax.broadcasted_iota(jnp.int32, (C, 2 * C), 1) & (C - 1))).astype(BF16)
    lw_hi = lw.astype(BF16)
    lw_lo = (lw - lw_hi.astype(F32)).astype(BF16)
    c = jnp.concatenate(
        [_dot(tri, jnp.concatenate([lw_hi[j * C:(j + 1) * C], lw_lo[j * C:(j + 1) * C]], axis=0))
         for j in range(nch)], axis=0)
    c3 = c.reshape(nch, C, D_MODEL)
    c_last = c3[:, C - 1:C, :]
    e_cc = jnp.exp(c_last - c3).reshape(ta, D_MODEL)
    dec_out[0] = jnp.exp(c_last)
    e_nc = jnp.exp(-c)
    k = k_ref[0].astype(F32)
    b = b_ref[0].astype(F32)
    at_s[...] = (-kk_ref[0].astype(F32) * jnp.exp(c - lw)).astype(BF16)
    rt_s[...] = (r_ref[0].astype(F32) * jnp.exp(c)).astype(BF16)
    kh_s[...] = (k * e_nc).astype(BF16)
    bh_s[...] = (b * e_nc).astype(BF16)
    kb_s[...] = (k * e_cc).astype(BF16)
    bb_s[...] = (b * e_cc).astype(BF16)

    t_idx = lax.broadcasted_iota(jnp.int32, (C, LANES), 0)
    s_idx = lax.broadcasted_iota(jnp.int32, (C, LANES), 1) & (HEAD_DIM - 1)
    strict = s_idx < t_idx
    incl = s_idx <= t_idx
    eye = (s_idx == t_idx).astype(F32)

    def chunk_body(j, carry):
        pairs = range(CHUNKS_PER_ITER * N_PAIRS)
        chunk = [CHUNKS_PER_ITER * j + u // N_PAIRS for u in pairs]
        rows = [pl.ds(pl.multiple_of(chunk[u] * C, C), C) for u in pairs]
        lanes = [slice((u % N_PAIRS) * LANES, (u % N_PAIRS + 1) * LANES) for u in pairs]
        at = [at_s[rows[p], lanes[p]] for p in pairs]
        rt = [rt_s[rows[p], lanes[p]] for p in pairs]
        ar = [jnp.concatenate([at[p], rt[p]], axis=0) for p in pairs]
        gk = [_dot_nt(ar[p], _block_diag(kh_s[rows[p], lanes[p]])) for p in pairs]
        gb = [_dot_nt(ar[p], _block_diag(bh_s[rows[p], lanes[p]])) for p in pairs]
        a_ak = [jnp.where(strict, gk[p][:C], 0.0).astype(BF16) for p in pairs]
        a_rk = [jnp.where(incl, gk[p][C:], 0.0).astype(BF16) for p in pairs]
        a_rb = [jnp.where(incl, gb[p][C:], 0.0).astype(BF16) for p in pairs]
        a_ab = [jnp.where(strict, gb[p][:C], 0.0) for p in pairs]
        tinv = [eye + a_ab[p] for p in pairs]
        npow = [a_ab[p].astype(BF16) for p in pairs]
        npow = [_dot(npow[p], _block_diag(npow[p])).astype(BF16) for p in pairs]
        for _ in range(4):
            res = [_dot(jnp.concatenate([npow[p], tinv[p].astype(BF16)], axis=0),
                        _block_diag(npow[p])) for p in pairs]
            tinv = [tinv[p] + res[p][C:] for p in pairs]
            npow = [res[p][:C].astype(BF16) for p in pairs]
        tinv = [(tinv[p] + _dot(tinv[p].astype(BF16), _block_diag(npow[p]))).astype(BF16)
                for p in pairs]
        vv = [v_ref[0, rows[p], lanes[p]] for p in pairs]
        x = [_dot(a_ak[p], _block_diag(vv[p])).astype(BF16) for p in pairs]
        wu = [_dot(tinv[p], jnp.concatenate([_block_diag(at[p]), _block_diag(x[p])], axis=1))
              for p in pairs]
        w_b = [wu[p][:, :LANES].astype(BF16) for p in pairs]
        u0_b = [wu[p][:, LANES:].astype(BF16) for p in pairs]
        for p in pairs:
            rw = rt[p].astype(F32) + _dot(a_rb[p], _block_diag(w_b[p]))
            rw_out[0, rows[p], lanes[p]] = rw.astype(rw_out.dtype)
        for p in pairs:
            y0_out[0, rows[p], lanes[p]] = _dot(
                jnp.concatenate([a_rk[p], a_rb[p]], axis=1),
                jnp.concatenate([_block_diag(vv[p]), _block_diag(u0_b[p])], axis=0))
        for p in pairs:
            bb = bb_s[rows[p], lanes[p]]
            mlr_out[0, chunk[p], :, lanes[p]] = _diag_blocks(
                _dot_tn(w_b[p], bb)).astype(mlr_out.dtype)
            sadd_out[0, chunk[p], :, lanes[p]] = _diag_blocks(
                _dot_tn(jnp.concatenate([vv[p], u0_b[p]], axis=0),
                        jnp.concatenate([kb_s[rows[p], lanes[p]], bb], axis=0)))
        return carry

    lax.fori_loop(0, nch // CHUNKS_PER_ITER, chunk_body, 0)


def _rwkv_chunk(r, lw, k, v, kk, b):
    bsz, lp, d = r.shape
    ta = TA_ROWS
    nch = ta // CHUNK
    nc = lp // CHUNK
    tile = pl.BlockSpec((1, ta, d), lambda bi, i: (bi, i, 0))
    per_chunk = lambda rows: pl.BlockSpec((1, nch, rows, d), lambda bi, i: (bi, i, 0, 0))
    return pl.pallas_call(
        _chunk_kernel,
        grid=(bsz, lp // ta),
        in_specs=[tile] * 6,
        out_specs=[tile, tile, per_chunk(HEAD_DIM), per_chunk(HEAD_DIM), per_chunk(1)],
        out_shape=[
            jax.ShapeDtypeStruct((bsz, lp, d), BF16),
            jax.ShapeDtypeStruct((bsz, lp, d), F32),
            jax.ShapeDtypeStruct((bsz, nc, HEAD_DIM, d), BF16),
            jax.ShapeDtypeStruct((bsz, nc, HEAD_DIM, d), F32),
            jax.ShapeDtypeStruct((bsz, nc, 1, d), F32),
        ],
        scratch_shapes=[pltpu.VMEM((ta, d), BF16)] * 6,
        compiler_params=pltpu.CompilerParams(
            dimension_semantics=("parallel", "parallel"), vmem_limit_bytes=VMEM_LIMIT),
        name="rwkv_chunk",
    )(r, lw, k, v, kk, b)


def _state_kernel(rw_ref, y0_ref, mlr_ref, sadd_ref, dec_ref, bonus_ref, g_ref, hv_ref,
                  gsum_ref, gexp_ref, o_ref, s_ref, y_s):
    bsz = rw_ref.shape[0]

    @pl.when(pl.program_id(0) == 0)
    def _():
        s_ref[...] = jnp.zeros_like(s_ref)

    def seq_body(i, carry):
        for u in range(SEQS_PER_ITER * N_PAIRS):
            bi = SEQS_PER_ITER * i + u // N_PAIRS
            ls = slice((u % N_PAIRS) * LANES, (u % N_PAIRS + 1) * LANES)
            s0 = s_ref[bi, :, ls]
            s0b = s0.astype(BF16)
            y_s[bi, :, ls] = y0_ref[bi, :, ls] + _dot_nt(rw_ref[bi, :, ls], _block_diag(s0b))
            s_ref[bi, :, ls] = (s0 * dec_ref[bi, 0, :, ls]
                                + _dot(s0b, _block_diag(mlr_ref[bi, 0, :, ls]))
                                + sadd_ref[bi, 0, :, ls])
        return carry

    lax.fori_loop(0, bsz // SEQS_PER_ITER, seq_body, 0)

    y = y_s[...].reshape(bsz * CHUNK, D_MODEL)
    inv_n = 1.0 / HEAD_DIM
    mean = _dot(_split_bf16(_dot(y.astype(BF16), gsum_ref[...]) * inv_n), gexp_ref[...])
    yc = y - mean
    var = _dot((yc * yc).astype(BF16), gsum_ref[...]) * inv_n
    rstd = _dot(_split_bf16(lax.rsqrt(var + GN_EPS)), gexp_ref[...])
    yn = yc * rstd * hv_ref[0:1, :] + hv_ref[1:2, :]
    bonus = bonus_ref[...].reshape(bsz * CHUNK, D_MODEL).astype(F32)
    g = g_ref[...].reshape(bsz * CHUNK, D_MODEL).astype(F32)
    o_ref[...] = ((yn + bonus) * g).astype(o_ref.dtype).reshape(o_ref.shape)


def _rwkv_state(rw, y0, mlr, sadd, dec, bonus, g, hv, gsum, gexp):
    bsz, lp, d = rw.shape
    nc = lp // CHUNK
    tile = pl.BlockSpec((bsz, CHUNK, d), lambda n: (0, n, 0))
    per_chunk = lambda rows: pl.BlockSpec((bsz, 1, rows, d), lambda n: (0, n, 0, 0))
    const = lambda shape: pl.BlockSpec(shape, lambda n: (0, 0))
    return pl.pallas_call(
        _state_kernel,
        grid=(nc,),
        in_specs=[tile, tile, per_chunk(HEAD_DIM), per_chunk(HEAD_DIM), per_chunk(1), tile, tile,
                  const(hv.shape), const(gsum.shape), const(gexp.shape)],
        out_specs=tile,
        out_shape=jax.ShapeDtypeStruct((bsz, lp, d), BF16),
        scratch_shapes=[pltpu.VMEM((bsz, HEAD_DIM, d), F32), pltpu.VMEM((bsz, CHUNK, d), F32)],
        compiler_params=pltpu.CompilerParams(
            dimension_semantics=("arbitrary",), vmem_limit_bytes=VMEM_LIMIT),
        name="rwkv_state",
    )(rw, y0, mlr, sadd, dec, bonus, g, hv, gsum, gexp)


def _post_mlp_kernel(o_ref, h_ref, wo_ref, ln_ref, wup_ref, wdn_ref, out_ref):
    mix = _dot(o_ref[...], wo_ref[...])
    out_ref[...] = _deepnorm_mlp(h_ref[...], mix, ln_ref, wup_ref, wdn_ref)


def _post_mlp(o_flat, h_flat, wo, ln, wup, wdn):
    rows = h_flat.shape[0]
    tm = TM_ROWS
    const = lambda shape: pl.BlockSpec(shape, lambda i: (0, 0), pipeline_mode=pl.Buffered(1))
    tile = pl.BlockSpec((tm, D_MODEL), lambda i: (i, 0))
    return pl.pallas_call(
        _post_mlp_kernel,
        grid=(rows // tm,),
        in_specs=[tile, tile, const(wo.shape), const(ln.shape), const(wup.shape), const(wdn.shape)],
        out_specs=tile,
        out_shape=jax.ShapeDtypeStruct((rows, D_MODEL), F32),
        compiler_params=pltpu.CompilerParams(
            dimension_semantics=("parallel",), vmem_limit_bytes=VMEM_LIMIT),
        name="post_mlp",
    )(o_flat, h_flat, wo, ln, wup, wdn)


def _rope(t, cos, sin_signed):
    out = []
    lane = lax.broadcasted_iota(jnp.int32, (t.shape[0], LANES), 1)
    first_half = (lane % HEAD_DIM) < (HEAD_DIM // 2)
    for j in range(t.shape[1] // LANES):
        tj = t[:, j * LANES:(j + 1) * LANES]
        rot = jnp.where(first_half, pltpu.roll(tj, LANES - HEAD_DIM // 2, axis=1),
                        pltpu.roll(tj, HEAD_DIM // 2, axis=1))
        out.append(tj * cos + rot * sin_signed)
    return jnp.concatenate(out, axis=1)


def _swa_kernel(sink_ref, h_ref, hprev_ref, cos_ref, sin_ref, cosp_ref, sinp_ref,
                wq_ref, wkv_ref, wo_ref, ln_ref, wup_ref, wdn_ref, out_ref, o_scr):
    h = h_ref[0]
    tm = h.shape[0]
    hb = h.astype(BF16)
    hcat = jnp.concatenate([hprev_ref[0].astype(BF16), hb], axis=0)
    cos_cat = jnp.concatenate([cosp_ref[...], cos_ref[...]], axis=0)
    sin_cat = jnp.concatenate([sinp_ref[...], sin_ref[...]], axis=0)

    q = _rope(_dot(hb, wq_ref[...]), cos_ref[...], sin_ref[...]) * (HEAD_DIM ** -0.5)
    kv = _dot(hcat, wkv_ref[...])
    kdim = 2 * N_HEADS_KV * HEAD_DIM
    kr = _rope(kv[:, :kdim], cos_cat, sin_cat).astype(BF16)
    vb = kv[:, kdim:].astype(BF16)
    qb = q.astype(BF16)

    nq = GROUP * BLOCK
    first_row = BLOCK + pl.program_id(1) * tm
    qi = lax.broadcasted_iota(jnp.int32, (nq, 2 * BLOCK), 0) & (BLOCK - 1)
    kj = lax.broadcasted_iota(jnp.int32, (nq, 2 * BLOCK), 1)
    rel = BLOCK + qi - kj
    in_window = (rel >= 0) & (rel < WINDOW)
    head_of_row = lax.broadcasted_iota(jnp.int32, (nq, 1), 0) // BLOCK
    first = lax.broadcasted_iota(jnp.int32, (BLOCK, LANES), 1) < HEAD_DIM
    zero = jnp.zeros((BLOCK, LANES), BF16)

    groups = range(N_HEADS_KV)
    for j in range(tm // BLOCK):
        rows = slice(j * BLOCK, (j + 1) * BLOCK)
        keys = slice(j * BLOCK, (j + 2) * BLOCK)
        key_row = first_row + (j - 1) * BLOCK + kj
        mask = in_window & (key_row >= PAD_FRONT)
        scores = []
        for g in groups:
            qa = qb[rows, 2 * g * LANES:(2 * g + 1) * LANES]
            qc = qb[rows, (2 * g + 1) * LANES:(2 * g + 2) * LANES]
            qs = jnp.concatenate([jnp.where(first, qa, zero), jnp.where(first, zero, qa),
                                  jnp.where(first, qc, zero), jnp.where(first, zero, qc)], axis=0)
            scores.append(_dot_nt(qs, kr[keys, g * LANES:(g + 1) * LANES]))
        probs, denoms = [], []
        for g in groups:
            s = jnp.where(mask, scores[g], -jnp.inf)
            sink = jnp.full((nq, 1), sink_ref[GROUP * g], F32)
            for i in range(1, GROUP):
                sink = jnp.where(head_of_row == i, sink_ref[GROUP * g + i], sink)
            m = jnp.maximum(jnp.max(s, axis=-1, keepdims=True), sink)
            p = jnp.exp(s - m)
            denoms.append(jnp.sum(p, axis=-1, keepdims=True) + jnp.exp(sink - m))
            probs.append(p.astype(BF16))
        for g in groups:
            o4 = _dot(probs[g], vb[keys, g * LANES:(g + 1) * LANES]) / denoms[g]
            o_scr[rows, 2 * g * LANES:(2 * g + 1) * LANES] = jnp.where(
                first, o4[:BLOCK], o4[BLOCK:2 * BLOCK]).astype(BF16)
            o_scr[rows, (2 * g + 1) * LANES:(2 * g + 2) * LANES] = jnp.where(
                first, o4[2 * BLOCK:3 * BLOCK], o4[3 * BLOCK:]).astype(BF16)

    mix = _dot(o_scr[...], wo_ref[...])
    out_ref[0] = _deepnorm_mlp(h, mix, ln_ref, wup_ref, wdn_ref)


def _swa_layer(h, sinks, cos, sin_signed, wq, wkv, wo, ln, wup, wdn):
    bsz, lp, d = h.shape
    tm = TM_ATT
    nblk = tm // BLOCK
    seq = lp - BLOCK
    const = lambda shape: pl.BlockSpec(shape, lambda bi, i: (0, 0), pipeline_mode=pl.Buffered(1))
    h_tile = pl.BlockSpec((pl.Element(1), pl.Element(tm), pl.Element(d)),
                          lambda bi, i: (bi, pl.multiple_of(BLOCK + i * tm, BLOCK), 0))
    table_tile = pl.BlockSpec((pl.Element(tm), pl.Element(LANES)),
                              lambda bi, i: (pl.multiple_of(BLOCK + i * tm, BLOCK), 0))
    return pl.pallas_call(
        _swa_kernel,
        grid=(bsz, seq // tm),
        in_specs=[
            pl.BlockSpec(memory_space=pltpu.SMEM),
            h_tile,
            pl.BlockSpec((1, BLOCK, d), lambda bi, i: (bi, i * nblk, 0)),
            table_tile,
            table_tile,
            pl.BlockSpec((BLOCK, LANES), lambda bi, i: (i * nblk, 0)),
            pl.BlockSpec((BLOCK, LANES), lambda bi, i: (i * nblk, 0)),
            const(wq.shape), const(wkv.shape), const(wo.shape), const(ln.shape),
            const(wup.shape), const(wdn.shape),
        ],
        out_specs=pl.BlockSpec((1, tm, d), lambda bi, i: (bi, i, 0)),
        out_shape=jax.ShapeDtypeStruct((bsz, seq, d), F32),
        scratch_shapes=[pltpu.VMEM((tm, d), BF16)],
        compiler_params=pltpu.CompilerParams(
            dimension_semantics=("parallel", "parallel"), vmem_limit_bytes=VMEM_LIMIT),
        name="swa_layer",
    )(sinks, h, h, cos, sin_signed, cos, sin_signed, wq, wkv, wo, ln, wup, wdn)


def kernel(x, meta_tokens, a_mu, a_w_r, a_w_k, a_w_v, a_w_o, a_w0, a_w1, a_w2, a_a0, a_a1, a_a2,
           a_g1, a_g2, a_k_k, a_k_a, a_r_k, a_gn_w, a_gn_b, kv_w_k, kv_w_v, b_w_q, b_sinks, b_w_o,
           mlp_w_up, mlp_w_down, ln_g, ln_b):
    bsz, seq, d = x.shape
    lp = PAD_FRONT + N_META + seq
    bf = lambda t: t.astype(BF16)

    meta = jnp.broadcast_to(meta_tokens[None].astype(x.dtype), (bsz, N_META, d))
    h0 = jnp.concatenate([jnp.zeros((bsz, PAD_FRONT, d), x.dtype), meta, x], axis=1)
    h0_flat = h0.reshape(bsz * lp, d)

    head_of_lane = jnp.arange(d) // HEAD_DIM
    gsum = (head_of_lane[:, None] == jnp.arange(LANES)[None, :]).astype(BF16)
    gexp = jnp.concatenate([gsum.T, gsum.T], axis=0)

    vec = jnp.stack([a_w0[0], a_a0[0], a_k_k[0], a_k_a[0], a_r_k[0].reshape(d)])
    r, lw, k, v, kk, b, g, bonus = _rwkv_proj(
        h0_flat, lp, a_mu[0], vec, bf(a_w_r[0]), bf(a_w_k[0]), bf(a_w_v[0]),
        bf(a_w1[0]), bf(a_w2[0]), bf(a_a1[0]), bf(a_a2[0]), bf(a_g1[0]), bf(a_g2[0]), gsum, gexp)

    shp = (bsz, lp, d)
    rw, y0, mlr, sadd, dec = _rwkv_chunk(*(t.reshape(shp) for t in (r, lw, k, v, kk, b)))
    hv = jnp.stack([a_gn_w[0], a_gn_b[0]])
    o = _rwkv_state(rw, y0, mlr, sadd, dec, bonus.reshape(shp), g.reshape(shp), hv, gsum, gexp)

    ln0 = jnp.stack([ln_g[0, 0], ln_b[0, 0], ln_g[0, 1], ln_b[0, 1]])
    h1 = _post_mlp(o.reshape(bsz * lp, d), h0_flat, bf(a_w_o[0]), ln0,
                   bf(mlp_w_up[0]), bf(mlp_w_down[0]))

    pos = (jnp.arange(lp) - PAD_FRONT).astype(F32)
    inv_freq = 1.0 / (ROPE_THETA ** (jnp.arange(0, HEAD_DIM, 2, dtype=F32) / HEAD_DIM))
    ang = pos[:, None] * inv_freq[None, :]
    cos = jnp.tile(jnp.cos(ang), (1, 4))
    sin_signed = jnp.tile(jnp.concatenate([-jnp.sin(ang), jnp.sin(ang)], axis=1), (1, 2))

    ln1 = jnp.stack([ln_g[1, 0], ln_b[1, 0], ln_g[1, 1], ln_b[1, 1]])
    twice = lambda w: jnp.tile(w.reshape(d, N_HEADS_KV, 1, HEAD_DIM), (1, 1, 2, 1)).reshape(d, -1)
    wkv = bf(jnp.concatenate([twice(kv_w_k), twice(kv_w_v)], axis=1))
    return _swa_layer(h1.reshape(shp), b_sinks[0], cos, sin_signed, bf(b_w_q[0]), wkv, bf(b_w_o[0]),
                      ln1, bf(mlp_w_up[1]), bf(mlp_w_down[1]))
```

```python
import jax
import jax.numpy as jnp
from jax import lax
from jax.experimental import pallas as pl
from jax.experimental.pallas import tpu as pltpu

D_MODEL = 1024
N_META = 16
HEAD_DIM = 64
N_HEADS = D_MODEL // HEAD_DIM
N_HEADS_KV = 4
GROUP = N_HEADS // N_HEADS_KV
GN_EPS = 64e-5
WINDOW = 128
BLOCK = 128
PAD_FRONT = BLOCK - N_META
ROPE_THETA = 10000.0
D_FF = 4 * D_MODEL
DEPTH = 2
ALPHA = (2.0 * DEPTH) ** 0.25
LN_EPS = 1e-5

LANES = 128
N_PAIRS = D_MODEL // LANES
CHUNK = 64
TM_ROWS = 512
TA_ROWS = 384
CHUNKS_PER_ITER = 2
SEQS_PER_ITER = 2
TM_ATT = 512
FF_CHUNK = 1024
VMEM_LIMIT = 56 * 1024 * 1024

BF16 = jnp.bfloat16
F32 = jnp.float32


def _dot(a, b):
    return jnp.dot(a, b, preferred_element_type=F32)


def _dot_nt(a, b):
    return lax.dot_general(a, b, (((1,), (1,)), ((), ())), preferred_element_type=F32)


def _dot_tn(a, b):
    return lax.dot_general(a, b, (((0,), (0,)), ((), ())), preferred_element_type=F32)


def _split_bf16(x):
    hi = x.astype(BF16)
    lo = (x - hi.astype(F32)).astype(BF16)
    return jnp.concatenate([hi, lo], axis=1)


def _block_diag(x):
    first = lax.broadcasted_iota(jnp.int32, x.shape, 1) < HEAD_DIM
    zero = jnp.zeros_like(x)
    return jnp.concatenate([jnp.where(first, x, zero), jnp.where(first, zero, x)], axis=0)


def _diag_blocks(full):
    first = lax.broadcasted_iota(jnp.int32, (HEAD_DIM, LANES), 1) < HEAD_DIM
    return jnp.where(first, full[:HEAD_DIM], full[HEAD_DIM:])


def _layer_norm(x, g, b):
    mu = jnp.mean(x, axis=-1, keepdims=True)
    xc = x - mu
    var = jnp.mean(xc * xc, axis=-1, keepdims=True)
    return xc * lax.rsqrt(var + LN_EPS) * g + b


def _deepnorm_mlp(h, mix, ln_ref, wup_ref, wdn_ref):
    h1 = _layer_norm(ALPHA * h + mix, ln_ref[0:1, :], ln_ref[1:2, :])
    hb = h1.astype(BF16)
    acc = jnp.zeros_like(h1)
    for c in range(D_FF // FF_CHUNK):
        up = _dot(hb, wup_ref[:, c * FF_CHUNK:(c + 1) * FF_CHUNK])
        up = jnp.maximum(up, 0.0)
        acc = acc + _dot((up * up).astype(BF16), wdn_ref[c * FF_CHUNK:(c + 1) * FF_CHUNK, :])
    return _layer_norm(ALPHA * h1 + acc, ln_ref[2:3, :], ln_ref[3:4, :])


def _proj_kernel(x_ref, xprev_ref, head_ref, mu_ref, vec_ref, wr_ref, wk_ref, wv_ref,
                 w1_ref, w2_ref, a1_ref, a2_ref, g1_ref, g2_ref, gsum_ref, gexp_ref,
                 h_out, r_out, lw_out, k_out, v_out, kk_out, b_out, g_out, bonus_out):
    first_tile = pl.program_id(1) == 0
    xw = x_ref[0]
    tm = xw.shape[0]
    x = jnp.where(first_tile, jnp.concatenate([head_ref[...], xw[:tm - BLOCK]], axis=0), xw)
    row = lax.broadcasted_iota(jnp.int32, x.shape, 0)
    prev_row = jnp.where(first_tile, 0.0, xprev_ref[0, 7:8, :])
    x_prev = jnp.where(row == 0, prev_row, pltpu.roll(x, 1, axis=0))
    xx = x_prev - x
    h_out[0] = x

    def mix(i):
        return (x + xx * mu_ref[i:i + 1, :]).astype(BF16)

    r = _dot(mix(0), wr_ref[...])
    k = _dot(mix(2), wk_ref[...])
    v = _dot(mix(3), wv_ref[...])
    w_lora = _dot(jnp.tanh(_dot(mix(1), w1_ref[...])).astype(BF16), w2_ref[...])
    a_lora = _dot(_dot(mix(4), a1_ref[...]).astype(BF16), a2_ref[...])
    g = _dot(jax.nn.sigmoid(_dot(mix(5), g1_ref[...])).astype(BF16), g2_ref[...])

    w0 = vec_ref[0:1, :]
    a0 = vec_ref[1:2, :]
    k_k = vec_ref[2:3, :]
    k_a = vec_ref[3:4, :]
    r_k = vec_ref[4:5, :]
    w = -jax.nn.softplus(-(w0 + w_lora)) - 0.5
    lw = -jnp.exp(w)
    a = jax.nn.sigmoid(a0 + a_lora)

    kk = k * k_k
    ss = _dot((kk * kk).astype(BF16), gsum_ref[...])
    inv = lax.rsqrt(jnp.maximum(ss, 1e-24))
    kk = kk * _dot(_split_bf16(inv), gexp_ref[...])
    k2 = k * (1.0 + (a - 1.0) * k_a)
    rk_sum = _dot((r * k2 * r_k).astype(BF16), gsum_ref[...])

    lw_out[0] = lw
    r_out[0] = r.astype(r_out.dtype)
    k_out[0] = k2.astype(k_out.dtype)
    v_out[0] = v.astype(v_out.dtype)
    kk_out[0] = kk.astype(kk_out.dtype)
    b_out[0] = (kk * a).astype(b_out.dtype)
    g_out[0] = g.astype(g_out.dtype)
    bonus_out[0] = (_dot(_split_bf16(rk_sum), gexp_ref[...]) * v).astype(bonus_out.dtype)


def _rwkv_proj(x, head, mu, vec, wr, wk, wv, w1, w2, a1, a2, g1, g2, gsum, gexp):
    bsz, seq, d = x.shape
    lp = seq + BLOCK
    tm = TA_ROWS
    const = lambda shape: pl.BlockSpec(shape, lambda bi, i: (0, 0), pipeline_mode=pl.Buffered(1))
    tile = pl.BlockSpec((1, tm, d), lambda bi, i: (bi, i, 0))
    x_rows = pl.BlockSpec(
        (pl.Element(1), pl.Element(tm), pl.Element(d)),
        lambda bi, i: (bi, pl.multiple_of(jnp.maximum(i * tm - BLOCK, 0), BLOCK), 0))
    x_prev_rows = pl.BlockSpec(
        (pl.Element(1), pl.Element(8), pl.Element(d)),
        lambda bi, i: (bi, pl.multiple_of(jnp.maximum(i * tm - BLOCK - 8, 0), 8), 0))
    f32_sds = jax.ShapeDtypeStruct((bsz, lp, d), F32)
    bf16_sds = jax.ShapeDtypeStruct((bsz, lp, d), BF16)
    return pl.pallas_call(
        _proj_kernel,
        grid=(bsz, lp // tm),
        in_specs=[
            x_rows, x_prev_rows, const(head.shape),
            const(mu.shape), const(vec.shape), const(wr.shape), const(wk.shape), const(wv.shape),
            const(w1.shape), const(w2.shape), const(a1.shape), const(a2.shape),
            const(g1.shape), const(g2.shape), const(gsum.shape), const(gexp.shape),
        ],
        out_specs=[tile] * 9,
        out_shape=[f32_sds, bf16_sds, f32_sds] + [bf16_sds] * 6,
        compiler_params=pltpu.CompilerParams(
            dimension_semantics=("parallel", "parallel"), vmem_limit_bytes=VMEM_LIMIT),
        name="rwkv_proj",
    )(x, x, head, mu, vec, wr, wk, wv, w1, w2, a1, a2, g1, g2, gsum, gexp)


def _chunk_kernel(r_ref, lw_ref, k_ref, v_ref, kk_ref, b_ref,
                  rw_out, y0_out, mlr_out, sadd_out, dec_out,
                  at_s, rt_s, kh_s, bh_s, kb_s, bb_s):
    C = CHUNK
    ta = lw_ref.shape[1]
    nch = ta // C

    lw = lw_ref[0]
    tri = (lax.broadcasted_iota(jnp.int32, (C, 2 * C), 0)
           >= (lax.broadcasted_iota(jnp.int32, (C, 2 * C), 1) & (C - 1))).astype(BF16)
    lw_hi = lw.astype(BF16)
    lw_lo = (lw - lw_hi.astype(F32)).astype(BF16)
    c = jnp.concatenate(
        [_dot(tri, jnp.concatenate([lw_hi[j * C:(j + 1) * C], lw_lo[j * C:(j + 1) * C]], axis=0))
         for j in range(nch)], axis=0)
    c3 = c.reshape(nch, C, D_MODEL)
    c_last = c3[:, C - 1:C, :]
    e_cc = jnp.exp(c_last - c3).reshape(ta, D_MODEL)
    dec_out[0] = jnp.exp(c_last)
    e_nc = jnp.exp(-c)
    k = k_ref[0].astype(F32)
    b = b_ref[0].astype(F32)
    at_s[...] = (-kk_ref[0].astype(F32) * jnp.exp(c - lw)).astype(BF16)
    rt_s[...] = (r_ref[0].astype(F32) * jnp.exp(c)).astype(BF16)
    kh_s[...] = (k * e_nc).astype(BF16)
    bh_s[...] = (b * e_nc).astype(BF16)
    kb_s[...] = (k * e_cc).astype(BF16)
    bb_s[...] = (b * e_cc).astype(BF16)

    t_idx = lax.broadcasted_iota(jnp.int32, (C, LANES), 0)
    s_idx = lax.broadcasted_iota(jnp.int32, (C, LANES), 1) & (HEAD_DIM - 1)
    strict = s_idx < t_idx
    incl = s_idx <= t_idx
    eye = (s_idx == t_idx).astype(F32)

    def chunk_body(j, carry):
        pairs = range(CHUNKS_PER_ITER * N_PAIRS)
        chunk = [CHUNKS_PER_ITER * j + u // N_PAIRS for u in pairs]
        rows = [pl.ds(pl.multiple_of(chunk[u] * C, C), C) for u in pairs]
        lanes = [slice((u % N_PAIRS) * LANES, (u % N_PAIRS + 1) * LANES) for u in pairs]
        at = [at_s[rows[p], lanes[p]] for p in pairs]
        rt = [rt_s[rows[p], lanes[p]] for p in pairs]
        ar = [jnp.concatenate([at[p], rt[p]], axis=0) for p in pairs]
        gk = [_dot_nt(ar[p], _block_diag(kh_s[rows[p], lanes[p]])) for p in pairs]
        gb = [_dot_nt(ar[p], _block_diag(bh_s[rows[p], lanes[p]])) for p in pairs]
        a_ak = [jnp.where(strict, gk[p][:C], 0.0).astype(BF16) for p in pairs]
        a_rk = [jnp.where(incl, gk[p][C:], 0.0).astype(BF16) for p in pairs]
        a_rb = [jnp.where(incl, gb[p][C:], 0.0).astype(BF16) for p in pairs]
        a_ab = [jnp.where(strict, gb[p][:C], 0.0) for p in pairs]
        tinv = [eye + a_ab[p] for p in pairs]
        npow = [a_ab[p].astype(BF16) for p in pairs]
        npow = [_dot(npow[p], _block_diag(npow[p])).astype(BF16) for p in pairs]
        for _ in range(4):
            res = [_dot(jnp.concatenate([npow[p], tinv[p].astype(BF16)], axis=0),
                        _block_diag(npow[p])) for p in pairs]
            tinv = [tinv[p] + res[p][C:] for p in pairs]
            npow = [res[p][:C].astype(BF16) for p in pairs]
        tinv = [(tinv[p] + _dot(tinv[p].astype(BF16), _block_diag(npow[p]))).astype(BF16)
                for p in pairs]
        vv = [v_ref[0, rows[p], lanes[p]] for p in pairs]
        x = [_dot(a_ak[p], _block_diag(vv[p])).astype(BF16) for p in pairs]
        wu = [_dot(tinv[p], jnp.concatenate([_block_diag(at[p]), _block_diag(x[p])], axis=1))
              for p in pairs]
        w_b = [wu[p][:, :LANES].astype(BF16) for p in pairs]
        u0_b = [wu[p][:, LANES:].astype(BF16) for p in pairs]
        for p in pairs:
            rw = rt[p].astype(F32) + _dot(a_rb[p], _block_diag(w_b[p]))
            rw_out[0, rows[p], lanes[p]] = rw.astype(rw_out.dtype)
        for p in pairs:
            y0_out[0, rows[p], lanes[p]] = _dot(
                jnp.concatenate([a_rk[p], a_rb[p]], axis=1),
                jnp.concatenate([_block_diag(vv[p]), _block_diag(u0_b[p])], axis=0))
        for p in pairs:
            bb = bb_s[rows[p], lanes[p]]
            mlr_out[0, chunk[p], :, lanes[p]] = _diag_blocks(
                _dot_tn(w_b[p], bb)).astype(mlr_out.dtype)
            sadd_out[0, chunk[p], :, lanes[p]] = _diag_blocks(
                _dot_tn(jnp.concatenate([vv[p], u0_b[p]], axis=0),
                        jnp.concatenate([kb_s[rows[p], lanes[p]], bb], axis=0)))
        return carry

    lax.fori_loop(0, nch // CHUNKS_PER_ITER, chunk_body, 0)


def _rwkv_chunk(r, lw, k, v, kk, b):
    bsz, lp, d = r.shape
    ta = TA_ROWS
    nch = ta // CHUNK
    nc = lp // CHUNK
    tile = pl.BlockSpec((1, ta, d), lambda bi, i: (bi, i, 0))
    per_chunk = lambda rows: pl.BlockSpec((1, nch, rows, d), lambda bi, i: (bi, i, 0, 0))
    return pl.pallas_call(
        _chunk_kernel,
        grid=(bsz, lp // ta),
        in_specs=[tile] * 6,
        out_specs=[tile, tile, per_chunk(HEAD_DIM), per_chunk(HEAD_DIM), per_chunk(1)],
        out_shape=[
            jax.ShapeDtypeStruct((bsz, lp, d), BF16),
            jax.ShapeDtypeStruct((bsz, lp, d), F32),
            jax.ShapeDtypeStruct((bsz, nc, HEAD_DIM, d), BF16),
            jax.ShapeDtypeStruct((bsz, nc, HEAD_DIM, d), F32),
            jax.ShapeDtypeStruct((bsz, nc, 1, d), F32),
        ],
        scratch_shapes=[pltpu.VMEM((ta, d), BF16)] * 6,
        compiler_params=pltpu.CompilerParams(
            dimension_semantics=("parallel", "parallel"), vmem_limit_bytes=VMEM_LIMIT),
        name="rwkv_chunk",
    )(r, lw, k, v, kk, b)


def _state_kernel(rw_ref, y0_ref, mlr_ref, sadd_ref, dec_ref, bonus_ref, g_ref, hv_ref,
                  gsum_ref, gexp_ref, o_ref, s_ref, y_s):
    bsz = rw_ref.shape[0]

    @pl.when(pl.program_id(0) == 0)
    def _():
        s_ref[...] = jnp.zeros_like(s_ref)

    def seq_body(i, carry):
        for u in range(SEQS_PER_ITER * N_PAIRS):
            bi = SEQS_PER_ITER * i + u // N_PAIRS
            ls = slice((u % N_PAIRS) * LANES, (u % N_PAIRS + 1) * LANES)
            s0 = s_ref[bi, :, ls]
            s0b = s0.astype(BF16)
            y_s[bi, :, ls] = y0_ref[bi, :, ls] + _dot_nt(rw_ref[bi, :, ls], _block_diag(s0b))
            s_ref[bi, :, ls] = (s0 * dec_ref[bi, 0, :, ls]
                                + _dot(s0b, _block_diag(mlr_ref[bi, 0, :, ls]))
                                + sadd_ref[bi, 0, :, ls])
        return carry

    lax.fori_loop(0, bsz // SEQS_PER_ITER, seq_body, 0)

    y = y_s[...].reshape(bsz * CHUNK, D_MODEL)
    inv_n = 1.0 / HEAD_DIM
    mean = _dot(_split_bf16(_dot(y.astype(BF16), gsum_ref[...]) * inv_n), gexp_ref[...])
    yc = y - mean
    var = _dot((yc * yc).astype(BF16), gsum_ref[...]) * inv_n
    rstd = _dot(_split_bf16(lax.rsqrt(var + GN_EPS)), gexp_ref[...])
    yn = yc * rstd * hv_ref[0:1, :] + hv_ref[1:2, :]
    bonus = bonus_ref[...].reshape(bsz * CHUNK, D_MODEL).astype(F32)
    g = g_ref[...].reshape(bsz * CHUNK, D_MODEL).astype(F32)
    o_ref[...] = ((yn + bonus) * g).astype(o_ref.dtype).reshape(o_ref.shape)


def _rwkv_state(rw, y0, mlr, sadd, dec, bonus, g, hv, gsum, gexp):
    bsz, lp, d = rw.shape
    nc = lp // CHUNK
    tile = pl.BlockSpec((bsz, CHUNK, d), lambda n: (0, n, 0))
    per_chunk = lambda rows: pl.BlockSpec((bsz, 1, rows, d), lambda n: (0, n, 0, 0))
    const = lambda shape: pl.BlockSpec(shape, lambda n: (0, 0))
    return pl.pallas_call(
        _state_kernel,
        grid=(nc,),
        in_specs=[tile, tile, per_chunk(HEAD_DIM), per_chunk(HEAD_DIM), per_chunk(1), tile, tile,
                  const(hv.shape), const(gsum.shape), const(gexp.shape)],
        out_specs=tile,
        out_shape=jax.ShapeDtypeStruct((bsz, lp, d), BF16),
        scratch_shapes=[pltpu.VMEM((bsz, HEAD_DIM, d), F32), pltpu.VMEM((bsz, CHUNK, d), F32)],
        compiler_params=pltpu.CompilerParams(
            dimension_semantics=("arbitrary",), vmem_limit_bytes=VMEM_LIMIT),
        name="rwkv_state",
    )(rw, y0, mlr, sadd, dec, bonus, g, hv, gsum, gexp)


def _post_mlp_kernel(o_ref, h_ref, wo_ref, ln_ref, wup_ref, wdn_ref, out_ref):
    mix = _dot(o_ref[...], wo_ref[...])
    out_ref[...] = _deepnorm_mlp(h_ref[...], mix, ln_ref, wup_ref, wdn_ref)


def _post_mlp(o_flat, h_flat, wo, ln, wup, wdn):
    rows = h_flat.shape[0]
    tm = TM_ROWS
    const = lambda shape: pl.BlockSpec(shape, lambda i: (0, 0), pipeline_mode=pl.Buffered(1))
    tile = pl.BlockSpec((tm, D_MODEL), lambda i: (i, 0))
    return pl.pallas_call(
        _post_mlp_kernel,
        grid=(rows // tm,),
        in_specs=[tile, tile, const(wo.shape), const(ln.shape), const(wup.shape), const(wdn.shape)],
        out_specs=tile,
        out_shape=jax.ShapeDtypeStruct((rows, D_MODEL), F32),
        compiler_params=pltpu.CompilerParams(
            dimension_semantics=("parallel",), vmem_limit_bytes=VMEM_LIMIT),
        name="post_mlp",
    )(o_flat, h_flat, wo, ln, wup, wdn)


def _rope(t, cos, sin_signed):
    out = []
    lane = lax.broadcasted_iota(jnp.int32, (t.shape[0], LANES), 1)
    first_half = (lane % HEAD_DIM) < (HEAD_DIM // 2)
    for j in range(t.shape[1] // LANES):
        tj = t[:, j * LANES:(j + 1) * LANES]
        rot = jnp.where(first_half, pltpu.roll(tj, LANES - HEAD_DIM // 2, axis=1),
                        pltpu.roll(tj, HEAD_DIM // 2, axis=1))
        out.append(tj * cos + rot * sin_signed)
    return jnp.concatenate(out, axis=1)


def _swa_kernel(sink_ref, h_ref, hprev_ref, cos_ref, sin_ref, cosp_ref, sinp_ref,
                wq_ref, wkv_ref, wo_ref, ln_ref, wup_ref, wdn_ref, out_ref, o_scr):
    h = h_ref[0]
    tm = h.shape[0]
    hb = h.astype(BF16)
    hcat = jnp.concatenate([hprev_ref[0].astype(BF16), hb], axis=0)
    cos_cat = jnp.concatenate([cosp_ref[...], cos_ref[...]], axis=0)
    sin_cat = jnp.concatenate([sinp_ref[...], sin_ref[...]], axis=0)

    q = _rope(_dot(hb, wq_ref[...]), cos_ref[...], sin_ref[...]) * (HEAD_DIM ** -0.5)
    kv = _dot(hcat, wkv_ref[...])
    kdim = 2 * N_HEADS_KV * HEAD_DIM
    kr = _rope(kv[:, :kdim], cos_cat, sin_cat).astype(BF16)
    vb = kv[:, kdim:].astype(BF16)
    qb = q.astype(BF16)

    nq = GROUP * BLOCK
    first_row = BLOCK + pl.program_id(1) * tm
    qi = lax.broadcasted_iota(jnp.int32, (nq, 2 * BLOCK), 0) & (BLOCK - 1)
    kj = lax.broadcasted_iota(jnp.int32, (nq, 2 * BLOCK), 1)
    rel = BLOCK + qi - kj
    in_window = (rel >= 0) & (rel < WINDOW)
    head_of_row = lax.broadcasted_iota(jnp.int32, (nq, 1), 0) // BLOCK
    first = lax.broadcasted_iota(jnp.int32, (BLOCK, LANES), 1) < HEAD_DIM
    zero = jnp.zeros((BLOCK, LANES), BF16)

    groups = range(N_HEADS_KV)
    for j in range(tm // BLOCK):
        rows = slice(j * BLOCK, (j + 1) * BLOCK)
        keys = slice(j * BLOCK, (j + 2) * BLOCK)
        key_row = first_row + (j - 1) * BLOCK + kj
        mask = in_window & (key_row >= PAD_FRONT)
        scores = []
        for g in groups:
            qa = qb[rows, 2 * g * LANES:(2 * g + 1) * LANES]
            qc = qb[rows, (2 * g + 1) * LANES:(2 * g + 2) * LANES]
            qs = jnp.concatenate([jnp.where(first, qa, zero), jnp.where(first, zero, qa),
                                  jnp.where(first, qc, zero), jnp.where(first, zero, qc)], axis=0)
            scores.append(_dot_nt(qs, kr[keys, g * LANES:(g + 1) * LANES]))
        probs, denoms = [], []
        for g in groups:
            s = jnp.where(mask, scores[g], -jnp.inf)
            sink = jnp.full((nq, 1), sink_ref[GROUP * g], F32)
            for i in range(1, GROUP):
                sink = jnp.where(head_of_row == i, sink_ref[GROUP * g + i], sink)
            m = jnp.maximum(jnp.max(s, axis=-1, keepdims=True), sink)
            p = jnp.exp(s - m)
            denoms.append(jnp.sum(p, axis=-1, keepdims=True) + jnp.exp(sink - m))
            probs.append(p.astype(BF16))
        for g in groups:
            o4 = _dot(probs[g], vb[keys, g * LANES:(g + 1) * LANES]) / denoms[g]
            o_scr[rows, 2 * g * LANES:(2 * g + 1) * LANES] = jnp.where(
                first, o4[:BLOCK], o4[BLOCK:2 * BLOCK]).astype(BF16)
            o_scr[rows, (2 * g + 1) * LANES:(2 * g + 2) * LANES] = jnp.where(
                first, o4[2 * BLOCK:3 * BLOCK], o4[3 * BLOCK:]).astype(BF16)

    mix = _dot(o_scr[...], wo_ref[...])
    out_ref[0] = _deepnorm_mlp(h, mix, ln_ref, wup_ref, wdn_ref)


def _swa_layer(h, sinks, cos, sin_signed, wq, wkv, wo, ln, wup, wdn):
    bsz, lp, d = h.shape
    tm = TM_ATT
    nblk = tm // BLOCK
    seq = lp - BLOCK
    const = lambda shape: pl.BlockSpec(shape, lambda bi, i: (0, 0), pipeline_mode=pl.Buffered(1))
    h_tile = pl.BlockSpec((pl.Element(1), pl.Element(tm), pl.Element(d)),
                          lambda bi, i: (bi, pl.multiple_of(BLOCK + i * tm, BLOCK), 0))
    table_tile = pl.BlockSpec((pl.Element(tm), pl.Element(LANES)),
                              lambda bi, i: (pl.multiple_of(BLOCK + i * tm, BLOCK), 0))
    return pl.pallas_call(
        _swa_kernel,
        grid=(bsz, seq // tm),
        in_specs=[
            pl.BlockSpec(memory_space=pltpu.SMEM),
            h_tile,
            pl.BlockSpec((1, BLOCK, d), lambda bi, i: (bi, i * nblk, 0)),
            table_tile,
            table_tile,
            pl.BlockSpec((BLOCK, LANES), lambda bi, i: (i * nblk, 0)),
            pl.BlockSpec((BLOCK, LANES), lambda bi, i: (i * nblk, 0)),
            const(wq.shape), const(wkv.shape), const(wo.shape), const(ln.shape),
            const(wup.shape), const(wdn.shape),
        ],
        out_specs=pl.BlockSpec((1, tm, d), lambda bi, i: (bi, i, 0)),
        out_shape=jax.ShapeDtypeStruct((bsz, seq, d), F32),
        scratch_shapes=[pltpu.VMEM((tm, d), BF16)],
        compiler_params=pltpu.CompilerParams(
            dimension_semantics=("parallel", "parallel"), vmem_limit_bytes=VMEM_LIMIT),
        name="swa_layer",
    )(sinks, h, h, cos, sin_signed, cos, sin_signed, wq, wkv, wo, ln, wup, wdn)


def kernel(x, meta_tokens, a_mu, a_w_r, a_w_k, a_w_v, a_w_o, a_w0, a_w1, a_w2, a_a0, a_a1, a_a2,
           a_g1, a_g2, a_k_k, a_k_a, a_r_k, a_gn_w, a_gn_b, kv_w_k, kv_w_v, b_w_q, b_sinks, b_w_o,
           mlp_w_up, mlp_w_down, ln_g, ln_b):
    bsz, seq, d = x.shape
    lp = PAD_FRONT + N_META + seq
    bf = lambda t: t.astype(BF16)

    head = jnp.concatenate([jnp.zeros((PAD_FRONT, d), x.dtype), meta_tokens.astype(x.dtype)], axis=0)

    head_of_lane = jnp.arange(d) // HEAD_DIM
    gsum = (head_of_lane[:, None] == jnp.arange(LANES)[None, :]).astype(BF16)
    gexp = jnp.concatenate([gsum.T, gsum.T], axis=0)

    vec = jnp.stack([a_w0[0], a_a0[0], a_k_k[0], a_k_a[0], a_r_k[0].reshape(d)])
    h0, r, lw, k, v, kk, b, g, bonus = _rwkv_proj(
        x, head, a_mu[0], vec, bf(a_w_r[0]), bf(a_w_k[0]), bf(a_w_v[0]),
        bf(a_w1[0]), bf(a_w2[0]), bf(a_a1[0]), bf(a_a2[0]), bf(a_g1[0]), bf(a_g2[0]), gsum, gexp)

    shp = (bsz, lp, d)
    rw, y0, mlr, sadd, dec = _rwkv_chunk(r, lw, k, v, kk, b)
    hv = jnp.stack([a_gn_w[0], a_gn_b[0]])
    o = _rwkv_state(rw, y0, mlr, sadd, dec, bonus, g, hv, gsum, gexp)

    ln0 = jnp.stack([ln_g[0, 0], ln_b[0, 0], ln_g[0, 1], ln_b[0, 1]])
    h1 = _post_mlp(o.reshape(bsz * lp, d), h0.reshape(bsz * lp, d), bf(a_w_o[0]), ln0,
                   bf(mlp_w_up[0]), bf(mlp_w_down[0]))

    pos = (jnp.arange(lp) - PAD_FRONT).astype(F32)
    inv_freq = 1.0 / (ROPE_THETA ** (jnp.arange(0, HEAD_DIM, 2, dtype=F32) / HEAD_DIM))
    ang = pos[:, None] * inv_freq[None, :]
    cos = jnp.tile(jnp.cos(ang), (1, 4))
    sin_signed = jnp.tile(jnp.concatenate([-jnp.sin(ang), jnp.sin(ang)], axis=1), (1, 2))

    ln1 = jnp.stack([ln_g[1, 0], ln_b[1, 0], ln_g[1, 1], ln_b[1, 1]])
    twice = lambda w: jnp.tile(w.reshape(d, N_HEADS_KV, 1, HEAD_DIM), (1, 1, 2, 1)).reshape(d, -1)
    wkv = bf(jnp.concatenate([twice(kv_w_k), twice(kv_w_v)], axis=1))
    return _swa_layer(h1.reshape(shp), b_sinks[0], cos, sin_signed, bf(b_w_q[0]), wkv, bf(b_w_o[0]),
                      ln1, bf(mlp_w_up[1]), bf(mlp_w_down[1]))
```

```python
import jax
import jax.numpy as jnp
from jax import lax
from jax.experimental import pallas as pl
from jax.experimental.pallas import tpu as pltpu

D_MODEL = 1024
N_META = 16
HEAD_DIM = 64
N_HEADS = D_MODEL // HEAD_DIM
N_HEADS_KV = 4
GROUP = N_HEADS // N_HEADS_KV
GN_EPS = 64e-5
WINDOW = 128
BLOCK = 128
PAD_FRONT = BLOCK - N_META
ROPE_THETA = 10000.0
D_FF = 4 * D_MODEL
DEPTH = 2
ALPHA = (2.0 * DEPTH) ** 0.25
LN_EPS = 1e-5

LANES = 128
N_PAIRS = D_MODEL // LANES
CHUNK = 64
TM_ROWS = 512
TA_ROWS = 384
CHUNKS_PER_ITER = 2
SEQS_PER_ITER = 2
TM_ATT = 512
ROW_SLABS = 2
FF_CHUNK = 1024
VMEM_LIMIT = 56 * 1024 * 1024

BF16 = jnp.bfloat16
F32 = jnp.float32


def _dot(a, b):
    return jnp.dot(a, b, preferred_element_type=F32)


def _dot_nt(a, b):
    return lax.dot_general(a, b, (((1,), (1,)), ((), ())), preferred_element_type=F32)


def _dot_tn(a, b):
    return lax.dot_general(a, b, (((0,), (0,)), ((), ())), preferred_element_type=F32)


def _split_bf16(x):
    hi = x.astype(BF16)
    lo = (x - hi.astype(F32)).astype(BF16)
    return jnp.concatenate([hi, lo], axis=1)


def _block_diag(x):
    first = lax.broadcasted_iota(jnp.int32, x.shape, 1) < HEAD_DIM
    zero = jnp.zeros_like(x)
    return jnp.concatenate([jnp.where(first, x, zero), jnp.where(first, zero, x)], axis=0)


def _diag_blocks(full):
    first = lax.broadcasted_iota(jnp.int32, (HEAD_DIM, LANES), 1) < HEAD_DIM
    return jnp.where(first, full[:HEAD_DIM], full[HEAD_DIM:])


def _layer_norm(x, g, b):
    mu = jnp.mean(x, axis=-1, keepdims=True)
    xc = x - mu
    var = jnp.mean(xc * xc, axis=-1, keepdims=True)
    return xc * lax.rsqrt(var + LN_EPS) * g + b


def _deepnorm_mlp(h, o, wo_ref, ln_ref, wup_ref, wdn_ref):
    n = h.shape[0] // ROW_SLABS
    slabs = range(ROW_SLABS)
    hs = [h[i * n:(i + 1) * n] for i in slabs]
    mix = [_dot(o[i * n:(i + 1) * n], wo_ref[...]) for i in slabs]
    h1 = [_layer_norm(ALPHA * hs[i] + mix[i], ln_ref[0:1, :], ln_ref[1:2, :]) for i in slabs]
    hb = [t.astype(BF16) for t in h1]
    acc = [jnp.zeros_like(t) for t in h1]
    for c in range(D_FF // FF_CHUNK):
        cols = slice(c * FF_CHUNK, (c + 1) * FF_CHUNK)
        up = [jnp.maximum(_dot(hb[i], wup_ref[:, cols]), 0.0) for i in slabs]
        acc = [acc[i] + _dot((up[i] * up[i]).astype(BF16), wdn_ref[cols, :]) for i in slabs]
    out = [_layer_norm(ALPHA * h1[i] + acc[i], ln_ref[2:3, :], ln_ref[3:4, :]) for i in slabs]
    return jnp.concatenate(out, axis=0)


def _proj_kernel(x_ref, xprev_ref, head_ref, mu_ref, vec_ref, wr_ref, wk_ref, wv_ref,
                 w1_ref, w2_ref, a1_ref, a2_ref, g1_ref, g2_ref, gsum_ref, gexp_ref,
                 h_out, r_out, lw_out, k_out, v_out, kk_out, b_out, g_out, bonus_out):
    first_tile = pl.program_id(1) == 0
    xw = x_ref[0]
    tm = xw.shape[0]
    x = jnp.where(first_tile, jnp.concatenate([head_ref[...], xw[:tm - BLOCK]], axis=0), xw)
    row = lax.broadcasted_iota(jnp.int32, x.shape, 0)
    prev_row = jnp.where(first_tile, 0.0, xprev_ref[0, 7:8, :])
    x_prev = jnp.where(row == 0, prev_row, pltpu.roll(x, 1, axis=0))
    xx = x_prev - x
    h_out[0] = x

    def mix(i):
        return (x + xx * mu_ref[i:i + 1, :]).astype(BF16)

    r = _dot(mix(0), wr_ref[...])
    k = _dot(mix(2), wk_ref[...])
    v = _dot(mix(3), wv_ref[...])
    w_lora = _dot(jnp.tanh(_dot(mix(1), w1_ref[...])).astype(BF16), w2_ref[...])
    a_lora = _dot(_dot(mix(4), a1_ref[...]).astype(BF16), a2_ref[...])
    g = _dot(jax.nn.sigmoid(_dot(mix(5), g1_ref[...])).astype(BF16), g2_ref[...])

    w0 = vec_ref[0:1, :]
    a0 = vec_ref[1:2, :]
    k_k = vec_ref[2:3, :]
    k_a = vec_ref[3:4, :]
    r_k = vec_ref[4:5, :]
    w = -jax.nn.softplus(-(w0 + w_lora)) - 0.5
    lw = -jnp.exp(w)
    a = jax.nn.sigmoid(a0 + a_lora)

    kk = k * k_k
    ss = _dot((kk * kk).astype(BF16), gsum_ref[...])
    inv = lax.rsqrt(jnp.maximum(ss, 1e-24))
    kk = kk * _dot(_split_bf16(inv), gexp_ref[...])
    k2 = k * (1.0 + (a - 1.0) * k_a)
    rk_sum = _dot((r * k2 * r_k).astype(BF16), gsum_ref[...])

    lw_out[0] = lw
    r_out[0] = r.astype(r_out.dtype)
    k_out[0] = k2.astype(k_out.dtype)
    v_out[0] = v.astype(v_out.dtype)
    kk_out[0] = kk.astype(kk_out.dtype)
    b_out[0] = (kk * a).astype(b_out.dtype)
    g_out[0] = g.astype(g_out.dtype)
    bonus_out[0] = (_dot(_split_bf16(rk_sum), gexp_ref[...]) * v).astype(bonus_out.dtype)


def _rwkv_proj(x, head, mu, vec, wr, wk, wv, w1, w2, a1, a2, g1, g2, gsum, gexp):
    bsz, seq, d = x.shape
    lp = seq + BLOCK
    tm = TA_ROWS
    const = lambda shape: pl.BlockSpec(shape, lambda bi, i: (0, 0), pipeline_mode=pl.Buffered(1))
    tile = pl.BlockSpec((1, tm, d), lambda bi, i: (bi, i, 0))
    x_rows = pl.BlockSpec(
        (pl.Element(1), pl.Element(tm), pl.Element(d)),
        lambda bi, i: (bi, pl.multiple_of(jnp.maximum(i * tm - BLOCK, 0), BLOCK), 0))
    x_prev_rows = pl.BlockSpec(
        (pl.Element(1), pl.Element(8), pl.Element(d)),
        lambda bi, i: (bi, pl.multiple_of(jnp.maximum(i * tm - BLOCK - 8, 0), 8), 0))
    f32_sds = jax.ShapeDtypeStruct((bsz, lp, d), F32)
    bf16_sds = jax.ShapeDtypeStruct((bsz, lp, d), BF16)
    return pl.pallas_call(
        _proj_kernel,
        grid=(bsz, lp // tm),
        in_specs=[
            x_rows, x_prev_rows, const(head.shape),
            const(mu.shape), const(vec.shape), const(wr.shape), const(wk.shape), const(wv.shape),
            const(w1.shape), const(w2.shape), const(a1.shape), const(a2.shape),
            const(g1.shape), const(g2.shape), const(gsum.shape), const(gexp.shape),
        ],
        out_specs=[tile] * 9,
        out_shape=[f32_sds, bf16_sds, f32_sds] + [bf16_sds] * 6,
        compiler_params=pltpu.CompilerParams(
            dimension_semantics=("parallel", "parallel"), vmem_limit_bytes=VMEM_LIMIT),
        name="rwkv_proj",
    )(x, x, head, mu, vec, wr, wk, wv, w1, w2, a1, a2, g1, g2, gsum, gexp)


def _chunk_kernel(r_ref, lw_ref, k_ref, v_ref, kk_ref, b_ref,
                  rw_out, y0_out, mlr_out, sadd_out, dec_out,
                  at_s, rt_s, kh_s, bh_s, kb_s, bb_s):
    C = CHUNK
    ta = lw_ref.shape[1]
    nch = ta // C

    lw = lw_ref[0]
    tri = (lax.broadcasted_iota(jnp.int32, (C, 2 * C), 0)
           >= (lax.broadcasted_iota(jnp.int32, (C, 2 * C), 1) & (C - 1))).astype(BF16)
    lw_hi = lw.astype(BF16)
    lw_lo = (lw - lw_hi.astype(F32)).astype(BF16)
    c = jnp.concatenate(
        [_dot(tri, jnp.concatenate([lw_hi[j * C:(j + 1) * C], lw_lo[j * C:(j + 1) * C]], axis=0))
         for j in range(nch)], axis=0)
    c3 = c.reshape(nch, C, D_MODEL)
    c_last = c3[:, C - 1:C, :]
    e_cc = jnp.exp(c_last - c3).reshape(ta, D_MODEL)
    dec_out[0] = jnp.exp(c_last)
    e_nc = jnp.exp(-c)
    k = k_ref[0].astype(F32)
    b = b_ref[0].astype(F32)
    at_s[...] = (-kk_ref[0].astype(F32) * jnp.exp(c - lw)).astype(BF16)
    rt_s[...] = (r_ref[0].astype(F32) * jnp.exp(c)).astype(BF16)
    kh_s[...] = (k * e_nc).astype(BF16)
    bh_s[...] = (b * e_nc).astype(BF16)
    kb_s[...] = (k * e_cc).astype(BF16)
    bb_s[...] = (b * e_cc).astype(BF16)

    t_idx = lax.broadcasted_iota(jnp.int32, (C, LANES), 0)
    s_idx = lax.broadcasted_iota(jnp.int32, (C, LANES), 1) & (HEAD_DIM - 1)
    strict = s_idx < t_idx
    incl = s_idx <= t_idx
    eye = (s_idx == t_idx).astype(F32)

    def chunk_body(j, carry):
        pairs = range(CHUNKS_PER_ITER * N_PAIRS)
        chunk = [CHUNKS_PER_ITER * j + u // N_PAIRS for u in pairs]
        rows = [pl.ds(pl.multiple_of(chunk[u] * C, C), C) for u in pairs]
        lanes = [slice((u % N_PAIRS) * LANES, (u % N_PAIRS + 1) * LANES) for u in pairs]
        at = [at_s[rows[p], lanes[p]] for p in pairs]
        rt = [rt_s[rows[p], lanes[p]] for p in pairs]
        ar = [jnp.concatenate([at[p], rt[p]], axis=0) for p in pairs]
        gk = [_dot_nt(ar[p], _block_diag(kh_s[rows[p], lanes[p]])) for p in pairs]
        gb = [_dot_nt(ar[p], _block_diag(bh_s[rows[p], lanes[p]])) for p in pairs]
        a_ak = [jnp.where(strict, gk[p][:C], 0.0).astype(BF16) for p in pairs]
        a_rk = [jnp.where(incl, gk[p][C:], 0.0).astype(BF16) for p in pairs]
        a_rb = [jnp.where(incl, gb[p][C:], 0.0).astype(BF16) for p in pairs]
        a_ab = [jnp.where(strict, gb[p][:C], 0.0) for p in pairs]
        tinv = [eye + a_ab[p] for p in pairs]
        npow = [a_ab[p].astype(BF16) for p in pairs]
        npow = [_dot(npow[p], _block_diag(npow[p])).astype(BF16) for p in pairs]
        for _ in range(4):
            res = [_dot(jnp.concatenate([npow[p], tinv[p].astype(BF16)], axis=0),
                        _block_diag(npow[p])) for p in pairs]
            tinv = [tinv[p] + res[p][C:] for p in pairs]
            npow = [res[p][:C].astype(BF16) for p in pairs]
        tinv = [(tinv[p] + _dot(tinv[p].astype(BF16), _block_diag(npow[p]))).astype(BF16)
                for p in pairs]
        vv = [v_ref[0, rows[p], lanes[p]] for p in pairs]
        x = [_dot(a_ak[p], _block_diag(vv[p])).astype(BF16) for p in pairs]
        wu = [_dot(tinv[p], jnp.concatenate([_block_diag(at[p]), _block_diag(x[p])], axis=1))
              for p in pairs]
        w_b = [wu[p][:, :LANES].astype(BF16) for p in pairs]
        u0_b = [wu[p][:, LANES:].astype(BF16) for p in pairs]
        for p in pairs:
            rw = rt[p].astype(F32) + _dot(a_rb[p], _block_diag(w_b[p]))
            rw_out[0, rows[p], lanes[p]] = rw.astype(rw_out.dtype)
        for p in pairs:
            y0_out[0, rows[p], lanes[p]] = _dot(
                jnp.concatenate([a_rk[p], a_rb[p]], axis=1),
                jnp.concatenate([_block_diag(vv[p]), _block_diag(u0_b[p])], axis=0))
        for p in pairs:
            bb = bb_s[rows[p], lanes[p]]
            mlr_out[0, chunk[p], :, lanes[p]] = _diag_blocks(
                _dot_tn(w_b[p], bb)).astype(mlr_out.dtype)
            sadd_out[0, chunk[p], :, lanes[p]] = _diag_blocks(
                _dot_tn(jnp.concatenate([vv[p], u0_b[p]], axis=0),
                        jnp.concatenate([kb_s[rows[p], lanes[p]], bb], axis=0)))
        return carry

    lax.fori_loop(0, nch // CHUNKS_PER_ITER, chunk_body, 0)


def _rwkv_chunk(r, lw, k, v, kk, b):
    bsz, lp, d = r.shape
    ta = TA_ROWS
    nch = ta // CHUNK
    nc = lp // CHUNK
    tile = pl.BlockSpec((1, ta, d), lambda bi, i: (bi, i, 0))
    per_chunk = lambda rows: pl.BlockSpec((1, nch, rows, d), lambda bi, i: (bi, i, 0, 0))
    return pl.pallas_call(
        _chunk_kernel,
        grid=(bsz, lp // ta),
        in_specs=[tile] * 6,
        out_specs=[tile, tile, per_chunk(HEAD_DIM), per_chunk(HEAD_DIM), per_chunk(1)],
        out_shape=[
            jax.ShapeDtypeStruct((bsz, lp, d), BF16),
            jax.ShapeDtypeStruct((bsz, lp, d), F32),
            jax.ShapeDtypeStruct((bsz, nc, HEAD_DIM, d), BF16),
            jax.ShapeDtypeStruct((bsz, nc, HEAD_DIM, d), F32),
            jax.ShapeDtypeStruct((bsz, nc, 1, d), F32),
        ],
        scratch_shapes=[pltpu.VMEM((ta, d), BF16)] * 6,
        compiler_params=pltpu.CompilerParams(
            dimension_semantics=("parallel", "parallel"), vmem_limit_bytes=VMEM_LIMIT),
        name="rwkv_chunk",
    )(r, lw, k, v, kk, b)


def _state_kernel(rw_ref, y0_ref, mlr_ref, sadd_ref, dec_ref, bonus_ref, g_ref, hv_ref,
                  gsum_ref, gexp_ref, o_ref, s_ref, y_s):
    bsz = rw_ref.shape[0]

    @pl.when(pl.program_id(0) == 0)
    def _():
        s_ref[...] = jnp.zeros_like(s_ref)

    def seq_body(i, carry):
        for u in range(SEQS_PER_ITER * N_PAIRS):
            bi = SEQS_PER_ITER * i + u // N_PAIRS
            ls = slice((u % N_PAIRS) * LANES, (u % N_PAIRS + 1) * LANES)
            s0 = s_ref[bi, :, ls]
            s0b = s0.astype(BF16)
            y_s[bi, :, ls] = y0_ref[bi, :, ls] + _dot_nt(rw_ref[bi, :, ls], _block_diag(s0b))
            s_ref[bi, :, ls] = (s0 * dec_ref[bi, 0, :, ls]
                                + _dot(s0b, _block_diag(mlr_ref[bi, 0, :, ls]))
                                + sadd_ref[bi, 0, :, ls])
        return carry

    lax.fori_loop(0, bsz // SEQS_PER_ITER, seq_body, 0)

    nb = bsz // ROW_SLABS
    slabs = range(ROW_SLABS)
    rows2d = lambda ref, i: ref[i * nb:(i + 1) * nb].reshape(nb * CHUNK, D_MODEL)
    inv_n = 1.0 / HEAD_DIM
    y = [rows2d(y_s, i) for i in slabs]
    s1 = [_dot(y[i].astype(BF16), gsum_ref[...]) * inv_n for i in slabs]
    yc = [y[i] - _dot(_split_bf16(s1[i]), gexp_ref[...]) for i in slabs]
    var = [_dot((yc[i] * yc[i]).astype(BF16), gsum_ref[...]) * inv_n for i in slabs]
    rstd = [_dot(_split_bf16(lax.rsqrt(var[i] + GN_EPS)), gexp_ref[...]) for i in slabs]
    for i in slabs:
        yn = yc[i] * rstd[i] * hv_ref[0:1, :] + hv_ref[1:2, :]
        o = (yn + rows2d(bonus_ref, i).astype(F32)) * rows2d(g_ref, i).astype(F32)
        o_ref[i * nb:(i + 1) * nb] = o.astype(o_ref.dtype).reshape(nb, CHUNK, D_MODEL)


def _rwkv_state(rw, y0, mlr, sadd, dec, bonus, g, hv, gsum, gexp):
    bsz, lp, d = rw.shape
    nc = lp // CHUNK
    tile = pl.BlockSpec((bsz, CHUNK, d), lambda n: (0, n, 0))
    per_chunk = lambda rows: pl.BlockSpec((bsz, 1, rows, d), lambda n: (0, n, 0, 0))
    const = lambda shape: pl.BlockSpec(shape, lambda n: (0, 0))
    return pl.pallas_call(
        _state_kernel,
        grid=(nc,),
        in_specs=[tile, tile, per_chunk(HEAD_DIM), per_chunk(HEAD_DIM), per_chunk(1), tile, tile,
                  const(hv.shape), const(gsum.shape), const(gexp.shape)],
        out_specs=tile,
        out_shape=jax.ShapeDtypeStruct((bsz, lp, d), BF16),
        scratch_shapes=[pltpu.VMEM((bsz, HEAD_DIM, d), F32), pltpu.VMEM((bsz, CHUNK, d), F32)],
        compiler_params=pltpu.CompilerParams(
            dimension_semantics=("arbitrary",), vmem_limit_bytes=VMEM_LIMIT),
        name="rwkv_state",
    )(rw, y0, mlr, sadd, dec, bonus, g, hv, gsum, gexp)


def _post_mlp_kernel(o_ref, h_ref, wo_ref, ln_ref, wup_ref, wdn_ref, out_ref):
    out_ref[...] = _deepnorm_mlp(h_ref[...], o_ref[...], wo_ref, ln_ref, wup_ref, wdn_ref)


def _post_mlp(o_flat, h_flat, wo, ln, wup, wdn):
    rows = h_flat.shape[0]
    tm = TM_ROWS
    const = lambda shape: pl.BlockSpec(shape, lambda i: (0, 0), pipeline_mode=pl.Buffered(1))
    tile = pl.BlockSpec((tm, D_MODEL), lambda i: (i, 0))
    return pl.pallas_call(
        _post_mlp_kernel,
        grid=(rows // tm,),
        in_specs=[tile, tile, const(wo.shape), const(ln.shape), const(wup.shape), const(wdn.shape)],
        out_specs=tile,
        out_shape=jax.ShapeDtypeStruct((rows, D_MODEL), F32),
        compiler_params=pltpu.CompilerParams(
            dimension_semantics=("parallel",), vmem_limit_bytes=VMEM_LIMIT),
        name="post_mlp",
    )(o_flat, h_flat, wo, ln, wup, wdn)


def _rope(t, cos, sin_signed):
    out = []
    lane = lax.broadcasted_iota(jnp.int32, (t.shape[0], LANES), 1)
    first_half = (lane % HEAD_DIM) < (HEAD_DIM // 2)
    for j in range(t.shape[1] // LANES):
        tj = t[:, j * LANES:(j + 1) * LANES]
        rot = jnp.where(first_half, pltpu.roll(tj, LANES - HEAD_DIM // 2, axis=1),
                        pltpu.roll(tj, HEAD_DIM // 2, axis=1))
        out.append(tj * cos + rot * sin_signed)
    return jnp.concatenate(out, axis=1)


def _swa_kernel(sink_ref, h_ref, hprev_ref, cos_ref, sin_ref, cosp_ref, sinp_ref,
                wq_ref, wkv_ref, wo_ref, ln_ref, wup_ref, wdn_ref, out_ref, o_scr):
    h = h_ref[0]
    tm = h.shape[0]
    hb = h.astype(BF16)
    hcat = jnp.concatenate([hprev_ref[0].astype(BF16), hb], axis=0)
    cos_cat = jnp.concatenate([cosp_ref[...], cos_ref[...]], axis=0)
    sin_cat = jnp.concatenate([sinp_ref[...], sin_ref[...]], axis=0)

    q = _rope(_dot(hb, wq_ref[...]), cos_ref[...], sin_ref[...]) * (HEAD_DIM ** -0.5)
    kv = _dot(hcat, wkv_ref[...])
    kdim = 2 * N_HEADS_KV * HEAD_DIM
    kr = _rope(kv[:, :kdim], cos_cat, sin_cat).astype(BF16)
    vb = kv[:, kdim:].astype(BF16)
    qb = q.astype(BF16)

    nq = GROUP * BLOCK
    first_row = BLOCK + pl.program_id(1) * tm
    qi = lax.broadcasted_iota(jnp.int32, (nq, 2 * BLOCK), 0) & (BLOCK - 1)
    kj = lax.broadcasted_iota(jnp.int32, (nq, 2 * BLOCK), 1)
    rel = BLOCK + qi - kj
    in_window = (rel >= 0) & (rel < WINDOW)
    head_of_row = lax.broadcasted_iota(jnp.int32, (nq, 1), 0) // BLOCK
    first = lax.broadcasted_iota(jnp.int32, (BLOCK, LANES), 1) < HEAD_DIM
    zero = jnp.zeros((BLOCK, LANES), BF16)

    groups = range(N_HEADS_KV)
    for j in range(tm // BLOCK):
        rows = slice(j * BLOCK, (j + 1) * BLOCK)
        keys = slice(j * BLOCK, (j + 2) * BLOCK)
        key_row = first_row + (j - 1) * BLOCK + kj
        mask = in_window & (key_row >= PAD_FRONT)
        scores = []
        for g in groups:
            qa = qb[rows, 2 * g * LANES:(2 * g + 1) * LANES]
            qc = qb[rows, (2 * g + 1) * LANES:(2 * g + 2) * LANES]
            qs = jnp.concatenate([jnp.where(first, qa, zero), jnp.where(first, zero, qa),
                                  jnp.where(first, qc, zero), jnp.where(first, zero, qc)], axis=0)
            scores.append(_dot_nt(qs, kr[keys, g * LANES:(g + 1) * LANES]))
        probs, denoms = [], []
        for g in groups:
            s = jnp.where(mask, scores[g], -jnp.inf)
            sink = jnp.full((nq, 1), sink_ref[GROUP * g], F32)
            for i in range(1, GROUP):
                sink = jnp.where(head_of_row == i, sink_ref[GROUP * g + i], sink)
            m = jnp.maximum(jnp.max(s, axis=-1, keepdims=True), sink)
            p = jnp.exp(s - m)
            denoms.append(jnp.sum(p, axis=-1, keepdims=True) + jnp.exp(sink - m))
            probs.append(p.astype(BF16))
        for g in groups:
            o4 = _dot(probs[g], vb[keys, g * LANES:(g + 1) * LANES]) / denoms[g]
            o_scr[rows, 2 * g * LANES:(2 * g + 1) * LANES] = jnp.where(
                first, o4[:BLOCK], o4[BLOCK:2 * BLOCK]).astype(BF16)
            o_scr[rows, (2 * g + 1) * LANES:(2 * g + 2) * LANES] = jnp.where(
                first, o4[2 * BLOCK:3 * BLOCK], o4[3 * BLOCK:]).astype(BF16)

    out_ref[0] = _deepnorm_mlp(h, o_scr[...], wo_ref, ln_ref, wup_ref, wdn_ref)


def _swa_layer(h, sinks, cos, sin_signed, wq, wkv, wo, ln, wup, wdn):
    bsz, lp, d = h.shape
    tm = TM_ATT
    nblk = tm // BLOCK
    seq = lp - BLOCK
    const = lambda shape: pl.BlockSpec(shape, lambda bi, i: (0, 0), pipeline_mode=pl.Buffered(1))
    h_tile = pl.BlockSpec((pl.Element(1), pl.Element(tm), pl.Element(d)),
                          lambda bi, i: (bi, pl.multiple_of(BLOCK + i * tm, BLOCK), 0))
    table_tile = pl.BlockSpec((pl.Element(tm), pl.Element(LANES)),
                              lambda bi, i: (pl.multiple_of(BLOCK + i * tm, BLOCK), 0))
    return pl.pallas_call(
        _swa_kernel,
        grid=(bsz, seq // tm),
        in_specs=[
            pl.BlockSpec(memory_space=pltpu.SMEM),
            h_tile,
            pl.BlockSpec((1, BLOCK, d), lambda bi, i: (bi, i * nblk, 0)),
            table_tile,
            table_tile,
            pl.BlockSpec((BLOCK, LANES), lambda bi, i: (i * nblk, 0)),
            pl.BlockSpec((BLOCK, LANES), lambda bi, i: (i * nblk, 0)),
            const(wq.shape), const(wkv.shape), const(wo.shape), const(ln.shape),
            const(wup.shape), const(wdn.shape),
        ],
        out_specs=pl.BlockSpec((1, tm, d), lambda bi, i: (bi, i, 0)),
        out_shape=jax.ShapeDtypeStruct((bsz, seq, d), F32),
        scratch_shapes=[pltpu.VMEM((tm, d), BF16)],
        compiler_params=pltpu.CompilerParams(
            dimension_semantics=("parallel", "parallel"), vmem_limit_bytes=VMEM_LIMIT),
        name="swa_layer",
    )(sinks, h, h, cos, sin_signed, cos, sin_signed, wq, wkv, wo, ln, wup, wdn)


def kernel(x, meta_tokens, a_mu, a_w_r, a_w_k, a_w_v, a_w_o, a_w0, a_w1, a_w2, a_a0, a_a1, a_a2,
           a_g1, a_g2, a_k_k, a_k_a, a_r_k, a_gn_w, a_gn_b, kv_w_k, kv_w_v, b_w_q, b_sinks, b_w_o,
           mlp_w_up, mlp_w_down, ln_g, ln_b):
    bsz, seq, d = x.shape
    lp = PAD_FRONT + N_META + seq
    bf = lambda t: t.astype(BF16)

    head = jnp.concatenate([jnp.zeros((PAD_FRONT, d), x.dtype), meta_tokens.astype(x.dtype)], axis=0)

    head_of_lane = jnp.arange(d) // HEAD_DIM
    gsum = (head_of_lane[:, None] == jnp.arange(LANES)[None, :]).astype(BF16)
    gexp = jnp.concatenate([gsum.T, gsum.T], axis=0)

    vec = jnp.stack([a_w0[0], a_a0[0], a_k_k[0], a_k_a[0], a_r_k[0].reshape(d)])
    h0, r, lw, k, v, kk, b, g, bonus = _rwkv_proj(
        x, head, a_mu[0], vec, bf(a_w_r[0]), bf(a_w_k[0]), bf(a_w_v[0]),
        bf(a_w1[0]), bf(a_w2[0]), bf(a_a1[0]), bf(a_a2[0]), bf(a_g1[0]), bf(a_g2[0]), gsum, gexp)

    shp = (bsz, lp, d)
    rw, y0, mlr, sadd, dec = _rwkv_chunk(r, lw, k, v, kk, b)
    hv = jnp.stack([a_gn_w[0], a_gn_b[0]])
    o = _rwkv_state(rw, y0, mlr, sadd, dec, bonus, g, hv, gsum, gexp)

    ln0 = jnp.stack([ln_g[0, 0], ln_b[0, 0], ln_g[0, 1], ln_b[0, 1]])
    h1 = _post_mlp(o.reshape(bsz * lp, d), h0.reshape(bsz * lp, d), bf(a_w_o[0]), ln0,
                   bf(mlp_w_up[0]), bf(mlp_w_down[0]))

    pos = (jnp.arange(lp) - PAD_FRONT).astype(F32)
    inv_freq = 1.0 / (ROPE_THETA ** (jnp.arange(0, HEAD_DIM, 2, dtype=F32) / HEAD_DIM))
    ang = pos[:, None] * inv_freq[None, :]
    cos = jnp.tile(jnp.cos(ang), (1, 4))
    sin_signed = jnp.tile(jnp.concatenate([-jnp.sin(ang), jnp.sin(ang)], axis=1), (1, 2))

    ln1 = jnp.stack([ln_g[1, 0], ln_b[1, 0], ln_g[1, 1], ln_b[1, 1]])
    twice = lambda w: jnp.tile(w.reshape(d, N_HEADS_KV, 1, HEAD_DIM), (1, 1, 2, 1)).reshape(d, -1)
    wkv = bf(jnp.concatenate([twice(kv_w_k), twice(kv_w_v)], axis=1))
    return _swa_layer(h1.reshape(shp), b_sinks[0], cos, sin_signed, bf(b_w_q[0]), wkv, bf(b_w_o[0]),
                      ln1, bf(mlp_w_up[1]), bf(mlp_w_down[1]))
```

```python
import jax
import jax.numpy as jnp
from jax import lax
from jax.experimental import pallas as pl
from jax.experimental.pallas import tpu as pltpu

D_MODEL = 1024
N_META = 16
HEAD_DIM = 64
N_HEADS = D_MODEL // HEAD_DIM
N_HEADS_KV = 4
GROUP = N_HEADS // N_HEADS_KV
GN_EPS = 64e-5
WINDOW = 128
BLOCK = 128
PAD_FRONT = BLOCK - N_META
ROPE_THETA = 10000.0
D_FF = 4 * D_MODEL
DEPTH = 2
ALPHA = (2.0 * DEPTH) ** 0.25
LN_EPS = 1e-5

LANES = 128
N_PAIRS = D_MODEL // LANES
CHUNK = 64
TM_ROWS = 512
TA_ROWS = 384
CHUNKS_PER_ITER = 2
SEQS_PER_ITER = 4
TM_ATT = 512
ROW_SLABS = 2
FF_CHUNK = 1024
VMEM_LIMIT = 56 * 1024 * 1024

BF16 = jnp.bfloat16
F32 = jnp.float32


def _dot(a, b):
    return jnp.dot(a, b, preferred_element_type=F32)


def _dot_nt(a, b):
    return lax.dot_general(a, b, (((1,), (1,)), ((), ())), preferred_element_type=F32)


def _dot_tn(a, b):
    return lax.dot_general(a, b, (((0,), (0,)), ((), ())), preferred_element_type=F32)


def _split_bf16(x):
    hi = x.astype(BF16)
    lo = (x - hi.astype(F32)).astype(BF16)
    return jnp.concatenate([hi, lo], axis=1)


def _block_diag(x):
    first = lax.broadcasted_iota(jnp.int32, x.shape, 1) < HEAD_DIM
    zero = jnp.zeros_like(x)
    return jnp.concatenate([jnp.where(first, x, zero), jnp.where(first, zero, x)], axis=0)


def _diag_blocks(full):
    first = lax.broadcasted_iota(jnp.int32, (HEAD_DIM, LANES), 1) < HEAD_DIM
    return jnp.where(first, full[:HEAD_DIM], full[HEAD_DIM:])


def _layer_norm(x, g, b):
    mu = jnp.mean(x, axis=-1, keepdims=True)
    xc = x - mu
    var = jnp.mean(xc * xc, axis=-1, keepdims=True)
    return xc * lax.rsqrt(var + LN_EPS) * g + b


def _deepnorm_mlp(h, o, wo_ref, ln_ref, wup_ref, wdn_ref):
    n = h.shape[0] // ROW_SLABS
    slabs = range(ROW_SLABS)
    hs = [h[i * n:(i + 1) * n] for i in slabs]
    mix = [_dot(o[i * n:(i + 1) * n], wo_ref[...]) for i in slabs]
    h1 = [_layer_norm(ALPHA * hs[i] + mix[i], ln_ref[0:1, :], ln_ref[1:2, :]) for i in slabs]
    hb = [t.astype(BF16) for t in h1]
    acc = [jnp.zeros_like(t) for t in h1]
    for c in range(D_FF // FF_CHUNK):
        cols = slice(c * FF_CHUNK, (c + 1) * FF_CHUNK)
        up = [jnp.maximum(_dot(hb[i], wup_ref[:, cols]), 0.0) for i in slabs]
        acc = [acc[i] + _dot((up[i] * up[i]).astype(BF16), wdn_ref[cols, :]) for i in slabs]
    out = [_layer_norm(ALPHA * h1[i] + acc[i], ln_ref[2:3, :], ln_ref[3:4, :]) for i in slabs]
    return jnp.concatenate(out, axis=0)


def _proj_kernel(x_ref, xprev_ref, head_ref, mu_ref, vec_ref, wr_ref, wk_ref, wv_ref,
                 w1_ref, w2_ref, a1_ref, a2_ref, g1_ref, g2_ref, gsum_ref, gexp_ref,
                 h_out, r_out, lw_out, k_out, v_out, kk_out, b_out, g_out, bonus_out):
    first_tile = pl.program_id(1) == 0
    xw = x_ref[0]
    tm = xw.shape[0]
    x = jnp.where(first_tile, jnp.concatenate([head_ref[...], xw[:tm - BLOCK]], axis=0), xw)
    row = lax.broadcasted_iota(jnp.int32, x.shape, 0)
    prev_row = jnp.where(first_tile, 0.0, xprev_ref[0, 7:8, :])
    x_prev = jnp.where(row == 0, prev_row, pltpu.roll(x, 1, axis=0))
    xx = x_prev - x
    h_out[0] = x

    def mix(i):
        return (x + xx * mu_ref[i:i + 1, :]).astype(BF16)

    r = _dot(mix(0), wr_ref[...])
    k = _dot(mix(2), wk_ref[...])
    v = _dot(mix(3), wv_ref[...])
    w_lora = _dot(jnp.tanh(_dot(mix(1), w1_ref[...])).astype(BF16), w2_ref[...])
    a_lora = _dot(_dot(mix(4), a1_ref[...]).astype(BF16), a2_ref[...])
    g = _dot(jax.nn.sigmoid(_dot(mix(5), g1_ref[...])).astype(BF16), g2_ref[...])

    w0 = vec_ref[0:1, :]
    a0 = vec_ref[1:2, :]
    k_k = vec_ref[2:3, :]
    k_a = vec_ref[3:4, :]
    r_k = vec_ref[4:5, :]
    w = -jax.nn.softplus(-(w0 + w_lora)) - 0.5
    lw = -jnp.exp(w)
    a = jax.nn.sigmoid(a0 + a_lora)

    kk = k * k_k
    ss = _dot((kk * kk).astype(BF16), gsum_ref[...])
    inv = lax.rsqrt(jnp.maximum(ss, 1e-24))
    kk = kk * _dot(_split_bf16(inv), gexp_ref[...])
    k2 = k * (1.0 + (a - 1.0) * k_a)
    rk_sum = _dot((r * k2 * r_k).astype(BF16), gsum_ref[...])

    lw_out[0] = lw
    r_out[0] = r.astype(r_out.dtype)
    k_out[0] = k2.astype(k_out.dtype)
    v_out[0] = v.astype(v_out.dtype)
    kk_out[0] = kk.astype(kk_out.dtype)
    b_out[0] = (kk * a).astype(b_out.dtype)
    g_out[0] = g.astype(g_out.dtype)
    bonus_out[0] = (_dot(_split_bf16(rk_sum), gexp_ref[...]) * v).astype(bonus_out.dtype)


def _rwkv_proj(x, head, mu, vec, wr, wk, wv, w1, w2, a1, a2, g1, g2, gsum, gexp):
    bsz, seq, d = x.shape
    lp = seq + BLOCK
    tm = TA_ROWS
    const = lambda shape: pl.BlockSpec(shape, lambda bi, i: (0, 0), pipeline_mode=pl.Buffered(1))
    tile = pl.BlockSpec((1, tm, d), lambda bi, i: (bi, i, 0))
    x_rows = pl.BlockSpec(
        (pl.Element(1), pl.Element(tm), pl.Element(d)),
        lambda bi, i: (bi, pl.multiple_of(jnp.maximum(i * tm - BLOCK, 0), BLOCK), 0))
    x_prev_rows = pl.BlockSpec(
        (pl.Element(1), pl.Element(8), pl.Element(d)),
        lambda bi, i: (bi, pl.multiple_of(jnp.maximum(i * tm - BLOCK - 8, 0), 8), 0))
    f32_sds = jax.ShapeDtypeStruct((bsz, lp, d), F32)
    bf16_sds = jax.ShapeDtypeStruct((bsz, lp, d), BF16)
    return pl.pallas_call(
        _proj_kernel,
        grid=(bsz, lp // tm),
        in_specs=[
            x_rows, x_prev_rows, const(head.shape),
            const(mu.shape), const(vec.shape), const(wr.shape), const(wk.shape), const(wv.shape),
            const(w1.shape), const(w2.shape), const(a1.shape), const(a2.shape),
            const(g1.shape), const(g2.shape), const(gsum.shape), const(gexp.shape),
        ],
        out_specs=[tile] * 9,
        out_shape=[f32_sds, bf16_sds, f32_sds] + [bf16_sds] * 6,
        compiler_params=pltpu.CompilerParams(
            dimension_semantics=("parallel", "parallel"), vmem_limit_bytes=VMEM_LIMIT),
        name="rwkv_proj",
    )(x, x, head, mu, vec, wr, wk, wv, w1, w2, a1, a2, g1, g2, gsum, gexp)


def _chunk_kernel(r_ref, lw_ref, k_ref, v_ref, kk_ref, b_ref,
                  rw_out, y0_out, mlr_out, sadd_out, dec_out,
                  at_s, rt_s, kh_s, bh_s, kb_s, bb_s):
    C = CHUNK
    ta = lw_ref.shape[1]
    nch = ta // C

    lw = lw_ref[0]
    tri = (lax.broadcasted_iota(jnp.int32, (C, 2 * C), 0)
           >= (lax.broadcasted_iota(jnp.int32, (C, 2 * C), 1) & (C - 1))).astype(BF16)
    lw_hi = lw.astype(BF16)
    lw_lo = (lw - lw_hi.astype(F32)).astype(BF16)
    c = jnp.concatenate(
        [_dot(tri, jnp.concatenate([lw_hi[j * C:(j + 1) * C], lw_lo[j * C:(j + 1) * C]], axis=0))
         for j in range(nch)], axis=0)
    c3 = c.reshape(nch, C, D_MODEL)
    c_last = c3[:, C - 1:C, :]
    e_cc = jnp.exp(c_last - c3).reshape(ta, D_MODEL)
    dec_out[0] = jnp.exp(c_last)
    e_nc = jnp.exp(-c)
    k = k_ref[0].astype(F32)
    b = b_ref[0].astype(F32)
    at_s[...] = (-kk_ref[0].astype(F32) * jnp.exp(c - lw)).astype(BF16)
    rt_s[...] = (r_ref[0].astype(F32) * jnp.exp(c)).astype(BF16)
    kh_s[...] = (k * e_nc).astype(BF16)
    bh_s[...] = (b * e_nc).astype(BF16)
    kb_s[...] = (k * e_cc).astype(BF16)
    bb_s[...] = (b * e_cc).astype(BF16)

    t_idx = lax.broadcasted_iota(jnp.int32, (C, LANES), 0)
    s_idx = lax.broadcasted_iota(jnp.int32, (C, LANES), 1) & (HEAD_DIM - 1)
    strict = s_idx < t_idx
    incl = s_idx <= t_idx
    eye = (s_idx == t_idx).astype(F32)

    def chunk_body(j, carry):
        pairs = range(CHUNKS_PER_ITER * N_PAIRS)
        chunk = [CHUNKS_PER_ITER * j + u // N_PAIRS for u in pairs]
        rows = [pl.ds(pl.multiple_of(chunk[u] * C, C), C) for u in pairs]
        lanes = [slice((u % N_PAIRS) * LANES, (u % N_PAIRS + 1) * LANES) for u in pairs]
        at = [at_s[rows[p], lanes[p]] for p in pairs]
        rt = [rt_s[rows[p], lanes[p]] for p in pairs]
        ar = [jnp.concatenate([at[p], rt[p]], axis=0) for p in pairs]
        gk = [_dot_nt(ar[p], _block_diag(kh_s[rows[p], lanes[p]])) for p in pairs]
        gb = [_dot_nt(ar[p], _block_diag(bh_s[rows[p], lanes[p]])) for p in pairs]
        a_ak = [jnp.where(strict, gk[p][:C], 0.0).astype(BF16) for p in pairs]
        a_rk = [jnp.where(incl, gk[p][C:], 0.0).astype(BF16) for p in pairs]
        a_rb = [jnp.where(incl, gb[p][C:], 0.0).astype(BF16) for p in pairs]
        a_ab = [jnp.where(strict, gb[p][:C], 0.0) for p in pairs]
        tinv = [eye + a_ab[p] for p in pairs]
        npow = [a_ab[p].astype(BF16) for p in pairs]
        npow = [_dot(npow[p], _block_diag(npow[p])).astype(BF16) for p in pairs]
        for _ in range(4):
            res = [_dot(jnp.concatenate([npow[p], tinv[p].astype(BF16)], axis=0),
                        _block_diag(npow[p])) for p in pairs]
            tinv = [tinv[p] + res[p][C:] for p in pairs]
            npow = [res[p][:C].astype(BF16) for p in pairs]
        tinv = [(tinv[p] + _dot(tinv[p].astype(BF16), _block_diag(npow[p]))).astype(BF16)
                for p in pairs]
        vv = [v_ref[0, rows[p], lanes[p]] for p in pairs]
        x = [_dot(a_ak[p], _block_diag(vv[p])).astype(BF16) for p in pairs]
        wu = [_dot(tinv[p], jnp.concatenate([_block_diag(at[p]), _block_diag(x[p])], axis=1))
              for p in pairs]
        w_b = [wu[p][:, :LANES].astype(BF16) for p in pairs]
        u0_b = [wu[p][:, LANES:].astype(BF16) for p in pairs]
        for p in pairs:
            rw = rt[p].astype(F32) + _dot(a_rb[p], _block_diag(w_b[p]))
            rw_out[0, rows[p], lanes[p]] = rw.astype(rw_out.dtype)
        for p in pairs:
            y0_out[0, rows[p], lanes[p]] = _dot(
                jnp.concatenate([a_rk[p], a_rb[p]], axis=1),
                jnp.concatenate([_block_diag(vv[p]), _block_diag(u0_b[p])], axis=0))
        for p in pairs:
            bb = bb_s[rows[p], lanes[p]]
            mlr_out[0, chunk[p], :, lanes[p]] = _diag_blocks(
                _dot_tn(w_b[p], bb)).astype(mlr_out.dtype)
            sadd_out[0, chunk[p], :, lanes[p]] = _diag_blocks(
                _dot_tn(jnp.concatenate([vv[p], u0_b[p]], axis=0),
                        jnp.concatenate([kb_s[rows[p], lanes[p]], bb], axis=0)))
        return carry

    lax.fori_loop(0, nch // CHUNKS_PER_ITER, chunk_body, 0)


def _rwkv_chunk(r, lw, k, v, kk, b):
    bsz, lp, d = r.shape
    ta = TA_ROWS
    nch = ta // CHUNK
    nc = lp // CHUNK
    tile = pl.BlockSpec((1, ta, d), lambda bi, i: (bi, i, 0))
    per_chunk = lambda rows: pl.BlockSpec((1, nch, rows, d), lambda bi, i: (bi, i, 0, 0))
    return pl.pallas_call(
        _chunk_kernel,
        grid=(bsz, lp // ta),
        in_specs=[tile] * 6,
        out_specs=[tile, tile, per_chunk(HEAD_DIM), per_chunk(HEAD_DIM), per_chunk(1)],
        out_shape=[
            jax.ShapeDtypeStruct((bsz, lp, d), BF16),
            jax.ShapeDtypeStruct((bsz, lp, d), F32),
            jax.ShapeDtypeStruct((bsz, nc, HEAD_DIM, d), BF16),
            jax.ShapeDtypeStruct((bsz, nc, HEAD_DIM, d), F32),
            jax.ShapeDtypeStruct((bsz, nc, 1, d), F32),
        ],
        scratch_shapes=[pltpu.VMEM((ta, d), BF16)] * 6,
        compiler_params=pltpu.CompilerParams(
            dimension_semantics=("parallel", "parallel"), vmem_limit_bytes=VMEM_LIMIT),
        name="rwkv_chunk",
    )(r, lw, k, v, kk, b)


def _state_kernel(rw_ref, y0_ref, mlr_ref, sadd_ref, dec_ref, bonus_ref, g_ref, hv_ref,
                  gsum_ref, gexp_ref, o_ref, s_ref, y_s):
    bsz = rw_ref.shape[0]

    @pl.when(pl.program_id(0) == 0)
    def _():
        s_ref[...] = jnp.zeros_like(s_ref)

    def seq_body(i, carry):
        for u in range(SEQS_PER_ITER * N_PAIRS):
            bi = SEQS_PER_ITER * i + u // N_PAIRS
            ls = slice((u % N_PAIRS) * LANES, (u % N_PAIRS + 1) * LANES)
            s0 = s_ref[bi, :, ls]
            s0b = s0.astype(BF16)
            y_s[bi, :, ls] = y0_ref[bi, :, ls] + _dot_nt(rw_ref[bi, :, ls], _block_diag(s0b))
            s_ref[bi, :, ls] = (s0 * dec_ref[bi, 0, :, ls]
                                + _dot(s0b, _block_diag(mlr_ref[bi, 0, :, ls]))
                                + sadd_ref[bi, 0, :, ls])
        return carry

    lax.fori_loop(0, bsz // SEQS_PER_ITER, seq_body, 0)

    nb = bsz // ROW_SLABS
    slabs = range(ROW_SLABS)
    rows2d = lambda ref, i: ref[i * nb:(i + 1) * nb].reshape(nb * CHUNK, D_MODEL)
    inv_n = 1.0 / HEAD_DIM
    y = [rows2d(y_s, i) for i in slabs]
    s1 = [_dot(y[i].astype(BF16), gsum_ref[...]) * inv_n for i in slabs]
    yc = [y[i] - _dot(_split_bf16(s1[i]), gexp_ref[...]) for i in slabs]
    var = [_dot((yc[i] * yc[i]).astype(BF16), gsum_ref[...]) * inv_n for i in slabs]
    rstd = [_dot(_split_bf16(lax.rsqrt(var[i] + GN_EPS)), gexp_ref[...]) for i in slabs]
    for i in slabs:
        yn = yc[i] * rstd[i] * hv_ref[0:1, :] + hv_ref[1:2, :]
        o = (yn + rows2d(bonus_ref, i).astype(F32)) * rows2d(g_ref, i).astype(F32)
        o_ref[i * nb:(i + 1) * nb] = o.astype(o_ref.dtype).reshape(nb, CHUNK, D_MODEL)


def _rwkv_state(rw, y0, mlr, sadd, dec, bonus, g, hv, gsum, gexp):
    bsz, lp, d = rw.shape
    nc = lp // CHUNK
    tile = pl.BlockSpec((bsz, CHUNK, d), lambda n: (0, n, 0))
    per_chunk = lambda rows: pl.BlockSpec((bsz, 1, rows, d), lambda n: (0, n, 0, 0))
    const = lambda shape: pl.BlockSpec(shape, lambda n: (0, 0))
    return pl.pallas_call(
        _state_kernel,
        grid=(nc,),
        in_specs=[tile, tile, per_chunk(HEAD_DIM), per_chunk(HEAD_DIM), per_chunk(1), tile, tile,
                  const(hv.shape), const(gsum.shape), const(gexp.shape)],
        out_specs=tile,
        out_shape=jax.ShapeDtypeStruct((bsz, lp, d), BF16),
        scratch_shapes=[pltpu.VMEM((bsz, HEAD_DIM, d), F32), pltpu.VMEM((bsz, CHUNK, d), F32)],
        compiler_params=pltpu.CompilerParams(
            dimension_semantics=("arbitrary",), vmem_limit_bytes=VMEM_LIMIT),
        name="rwkv_state",
    )(rw, y0, mlr, sadd, dec, bonus, g, hv, gsum, gexp)


def _post_mlp_kernel(o_ref, h_ref, wo_ref, ln_ref, wup_ref, wdn_ref, out_ref):
    out_ref[...] = _deepnorm_mlp(h_ref[...], o_ref[...], wo_ref, ln_ref, wup_ref, wdn_ref)


def _post_mlp(o_flat, h_flat, wo, ln, wup, wdn):
    rows = h_flat.shape[0]
    tm = TM_ROWS
    const = lambda shape: pl.BlockSpec(shape, lambda i: (0, 0), pipeline_mode=pl.Buffered(1))
    tile = pl.BlockSpec((tm, D_MODEL), lambda i: (i, 0))
    return pl.pallas_call(
        _post_mlp_kernel,
        grid=(rows // tm,),
        in_specs=[tile, tile, const(wo.shape), const(ln.shape), const(wup.shape), const(wdn.shape)],
        out_specs=tile,
        out_shape=jax.ShapeDtypeStruct((rows, D_MODEL), F32),
        compiler_params=pltpu.CompilerParams(
            dimension_semantics=("parallel",), vmem_limit_bytes=VMEM_LIMIT),
        name="post_mlp",
    )(o_flat, h_flat, wo, ln, wup, wdn)


def _rope(t, cos, sin_signed):
    out = []
    lane = lax.broadcasted_iota(jnp.int32, (t.shape[0], LANES), 1)
    first_half = (lane % HEAD_DIM) < (HEAD_DIM // 2)
    for j in range(t.shape[1] // LANES):
        tj = t[:, j * LANES:(j + 1) * LANES]
        rot = jnp.where(first_half, pltpu.roll(tj, LANES - HEAD_DIM // 2, axis=1),
                        pltpu.roll(tj, HEAD_DIM // 2, axis=1))
        out.append(tj * cos + rot * sin_signed)
    return jnp.concatenate(out, axis=1)


def _swa_kernel(sink_ref, h_ref, hprev_ref, cos_ref, sin_ref, cosp_ref, sinp_ref,
                wq_ref, wkv_ref, wo_ref, ln_ref, wup_ref, wdn_ref, out_ref, o_scr):
    h = h_ref[0]
    tm = h.shape[0]
    hb = h.astype(BF16)
    hcat = jnp.concatenate([hprev_ref[0].astype(BF16), hb], axis=0)
    cos_cat = jnp.concatenate([cosp_ref[...], cos_ref[...]], axis=0)
    sin_cat = jnp.concatenate([sinp_ref[...], sin_ref[...]], axis=0)

    q = _rope(_dot(hb, wq_ref[...]), cos_ref[...], sin_ref[...]) * (HEAD_DIM ** -0.5)
    kv = _dot(hcat, wkv_ref[...])
    kdim = N_HEADS_KV * HEAD_DIM

    def twice(t):
        half = lax.broadcasted_iota(jnp.int32, (t.shape[0], LANES), 1) < HEAD_DIM
        out = []
        for j in range(t.shape[1] // LANES):
            tj = t[:, j * LANES:(j + 1) * LANES]
            sw = pltpu.roll(tj, HEAD_DIM, axis=1)
            out += [jnp.where(half, tj, sw), jnp.where(half, sw, tj)]
        return jnp.concatenate(out, axis=1)

    kr = twice(_rope(kv[:, :kdim], cos_cat, sin_cat)).astype(BF16)
    vb = twice(kv[:, kdim:]).astype(BF16)
    qb = q.astype(BF16)

    nq = GROUP * BLOCK
    first_row = BLOCK + pl.program_id(1) * tm
    qi = lax.broadcasted_iota(jnp.int32, (nq, 2 * BLOCK), 0) & (BLOCK - 1)
    kj = lax.broadcasted_iota(jnp.int32, (nq, 2 * BLOCK), 1)
    rel = BLOCK + qi - kj
    in_window = (rel >= 0) & (rel < WINDOW)
    head_of_row = lax.broadcasted_iota(jnp.int32, (nq, 1), 0) // BLOCK
    first = lax.broadcasted_iota(jnp.int32, (BLOCK, LANES), 1) < HEAD_DIM
    zero = jnp.zeros((BLOCK, LANES), BF16)

    groups = range(N_HEADS_KV)
    for j in range(tm // BLOCK):
        rows = slice(j * BLOCK, (j + 1) * BLOCK)
        keys = slice(j * BLOCK, (j + 2) * BLOCK)
        key_row = first_row + (j - 1) * BLOCK + kj
        mask = in_window & (key_row >= PAD_FRONT)
        scores = []
        for g in groups:
            qa = qb[rows, 2 * g * LANES:(2 * g + 1) * LANES]
            qc = qb[rows, (2 * g + 1) * LANES:(2 * g + 2) * LANES]
            qs = jnp.concatenate([jnp.where(first, qa, zero), jnp.where(first, zero, qa),
                                  jnp.where(first, qc, zero), jnp.where(first, zero, qc)], axis=0)
            scores.append(_dot_nt(qs, kr[keys, g * LANES:(g + 1) * LANES]))
        probs, denoms = [], []
        for g in groups:
            s = jnp.where(mask, scores[g], -jnp.inf)
            sink = jnp.full((nq, 1), sink_ref[GROUP * g], F32)
            for i in range(1, GROUP):
                sink = jnp.where(head_of_row == i, sink_ref[GROUP * g + i], sink)
            m = jnp.maximum(jnp.max(s, axis=-1, keepdims=True), sink)
            p = jnp.exp(s - m)
            denoms.append(jnp.sum(p, axis=-1, keepdims=True) + jnp.exp(sink - m))
            probs.append(p.astype(BF16))
        for g in groups:
            o4 = _dot(probs[g], vb[keys, g * LANES:(g + 1) * LANES]) / denoms[g]
            o_scr[rows, 2 * g * LANES:(2 * g + 1) * LANES] = jnp.where(
                first, o4[:BLOCK], o4[BLOCK:2 * BLOCK]).astype(BF16)
            o_scr[rows, (2 * g + 1) * LANES:(2 * g + 2) * LANES] = jnp.where(
                first, o4[2 * BLOCK:3 * BLOCK], o4[3 * BLOCK:]).astype(BF16)

    out_ref[0] = _deepnorm_mlp(h, o_scr[...], wo_ref, ln_ref, wup_ref, wdn_ref)


def _swa_layer(h, sinks, cos, sin_signed, wq, wkv, wo, ln, wup, wdn):
    bsz, lp, d = h.shape
    tm = TM_ATT
    nblk = tm // BLOCK
    seq = lp - BLOCK
    const = lambda shape: pl.BlockSpec(shape, lambda bi, i: (0, 0), pipeline_mode=pl.Buffered(1))
    h_tile = pl.BlockSpec((pl.Element(1), pl.Element(tm), pl.Element(d)),
                          lambda bi, i: (bi, pl.multiple_of(BLOCK + i * tm, BLOCK), 0))
    table_tile = pl.BlockSpec((pl.Element(tm), pl.Element(LANES)),
                              lambda bi, i: (pl.multiple_of(BLOCK + i * tm, BLOCK), 0))
    return pl.pallas_call(
        _swa_kernel,
        grid=(bsz, seq // tm),
        in_specs=[
            pl.BlockSpec(memory_space=pltpu.SMEM),
            h_tile,
            pl.BlockSpec((1, BLOCK, d), lambda bi, i: (bi, i * nblk, 0)),
            table_tile,
            table_tile,
            pl.BlockSpec((BLOCK, LANES), lambda bi, i: (i * nblk, 0)),
            pl.BlockSpec((BLOCK, LANES), lambda bi, i: (i * nblk, 0)),
            const(wq.shape), const(wkv.shape), const(wo.shape), const(ln.shape),
            const(wup.shape), const(wdn.shape),
        ],
        out_specs=pl.BlockSpec((1, tm, d), lambda bi, i: (bi, i, 0)),
        out_shape=jax.ShapeDtypeStruct((bsz, seq, d), F32),
        scratch_shapes=[pltpu.VMEM((tm, d), BF16)],
        compiler_params=pltpu.CompilerParams(
            dimension_semantics=("parallel", "parallel"), vmem_limit_bytes=VMEM_LIMIT),
        name="swa_layer",
    )(sinks, h, h, cos, sin_signed, cos, sin_signed, wq, wkv, wo, ln, wup, wdn)


def kernel(x, meta_tokens, a_mu, a_w_r, a_w_k, a_w_v, a_w_o, a_w0, a_w1, a_w2, a_a0, a_a1, a_a2,
           a_g1, a_g2, a_k_k, a_k_a, a_r_k, a_gn_w, a_gn_b, kv_w_k, kv_w_v, b_w_q, b_sinks, b_w_o,
           mlp_w_up, mlp_w_down, ln_g, ln_b):
    bsz, seq, d = x.shape
    lp = PAD_FRONT + N_META + seq
    bf = lambda t: t.astype(BF16)

    head = jnp.concatenate([jnp.zeros((PAD_FRONT, d), x.dtype), meta_tokens.astype(x.dtype)], axis=0)

    head_of_lane = jnp.arange(d) // HEAD_DIM
    gsum = (head_of_lane[:, None] == jnp.arange(LANES)[None, :]).astype(BF16)
    gexp = jnp.concatenate([gsum.T, gsum.T], axis=0)

    vec = jnp.stack([a_w0[0], a_a0[0], a_k_k[0], a_k_a[0], a_r_k[0].reshape(d)])
    h0, r, lw, k, v, kk, b, g, bonus = _rwkv_proj(
        x, head, a_mu[0], vec, bf(a_w_r[0]), bf(a_w_k[0]), bf(a_w_v[0]),
        bf(a_w1[0]), bf(a_w2[0]), bf(a_a1[0]), bf(a_a2[0]), bf(a_g1[0]), bf(a_g2[0]), gsum, gexp)

    shp = (bsz, lp, d)
    rw, y0, mlr, sadd, dec = _rwkv_chunk(r, lw, k, v, kk, b)
    hv = jnp.stack([a_gn_w[0], a_gn_b[0]])
    o = _rwkv_state(rw, y0, mlr, sadd, dec, bonus, g, hv, gsum, gexp)

    ln0 = jnp.stack([ln_g[0, 0], ln_b[0, 0], ln_g[0, 1], ln_b[0, 1]])
    h1 = _post_mlp(o.reshape(bsz * lp, d), h0.reshape(bsz * lp, d), bf(a_w_o[0]), ln0,
                   bf(mlp_w_up[0]), bf(mlp_w_down[0]))

    pos = (jnp.arange(lp) - PAD_FRONT).astype(F32)
    inv_freq = 1.0 / (ROPE_THETA ** (jnp.arange(0, HEAD_DIM, 2, dtype=F32) / HEAD_DIM))
    ang = pos[:, None] * inv_freq[None, :]
    cos = jnp.tile(jnp.cos(ang), (1, 4))
    sin_signed = jnp.tile(jnp.concatenate([-jnp.sin(ang), jnp.sin(ang)], axis=1), (1, 2))

    ln1 = jnp.stack([ln_g[1, 0], ln_b[1, 0], ln_g[1, 1], ln_b[1, 1]])
    wkv = bf(jnp.concatenate([kv_w_k, kv_w_v], axis=1))
    return _swa_layer(h1.reshape(shp), b_sinks[0], cos, sin_signed, bf(b_w_q[0]), wkv, bf(b_w_o[0]),
                      ln1, bf(mlp_w_up[1]), bf(mlp_w_down[1]))
```

```python
import jax
import jax.numpy as jnp
from jax import lax
from jax.experimental import pallas as pl
from jax.experimental.pallas import tpu as pltpu

D_MODEL = 1024
N_META = 16
HEAD_DIM = 64
N_HEADS = D_MODEL // HEAD_DIM
N_HEADS_KV = 4
GROUP = N_HEADS // N_HEADS_KV
GN_EPS = 64e-5
WINDOW = 128
BLOCK = 128
PAD_FRONT = BLOCK - N_META
ROPE_THETA = 10000.0
D_FF = 4 * D_MODEL
DEPTH = 2
ALPHA = (2.0 * DEPTH) ** 0.25
LN_EPS = 1e-5

LANES = 128
N_PAIRS = D_MODEL // LANES
CHUNK = 64
TM_ROWS = 512
TA_ROWS = 384
CHUNKS_PER_ITER = 2
SEQS_PER_ITER = 4
TM_ATT = 512
ROW_SLABS = 2
FF_CHUNK = 1024
VMEM_LIMIT = 56 * 1024 * 1024

BF16 = jnp.bfloat16
F32 = jnp.float32


def _dot(a, b):
    return jnp.dot(a, b, preferred_element_type=F32)


def _dot_nt(a, b):
    return lax.dot_general(a, b, (((1,), (1,)), ((), ())), preferred_element_type=F32)


def _dot_tn(a, b):
    return lax.dot_general(a, b, (((0,), (0,)), ((), ())), preferred_element_type=F32)


def _split_bf16(x):
    hi = x.astype(BF16)
    lo = (x - hi.astype(F32)).astype(BF16)
    return jnp.concatenate([hi, lo], axis=1)


def _block_diag(x):
    first = lax.broadcasted_iota(jnp.int32, x.shape, 1) < HEAD_DIM
    zero = jnp.zeros_like(x)
    return jnp.concatenate([jnp.where(first, x, zero), jnp.where(first, zero, x)], axis=0)


def _diag_blocks(full):
    first = lax.broadcasted_iota(jnp.int32, (HEAD_DIM, LANES), 1) < HEAD_DIM
    return jnp.where(first, full[:HEAD_DIM], full[HEAD_DIM:])


def _layer_norm(x, g, b):
    mu = jnp.mean(x, axis=-1, keepdims=True)
    xc = x - mu
    var = jnp.mean(xc * xc, axis=-1, keepdims=True)
    return xc * lax.rsqrt(var + LN_EPS) * g + b


def _deepnorm_mlp(h, o, wo_ref, ln_ref, wup_ref, wdn_ref):
    n = h.shape[0] // ROW_SLABS
    slabs = range(ROW_SLABS)
    hs = [h[i * n:(i + 1) * n] for i in slabs]
    mix = [_dot(o[i * n:(i + 1) * n], wo_ref[...]) for i in slabs]
    h1 = [_layer_norm(ALPHA * hs[i] + mix[i], ln_ref[0:1, :], ln_ref[1:2, :]) for i in slabs]
    hb = [t.astype(BF16) for t in h1]
    acc = [jnp.zeros_like(t) for t in h1]
    for c in range(D_FF // FF_CHUNK):
        cols = slice(c * FF_CHUNK, (c + 1) * FF_CHUNK)
        up = [jnp.maximum(_dot(hb[i], wup_ref[:, cols]), 0.0) for i in slabs]
        acc = [acc[i] + _dot((up[i] * up[i]).astype(BF16), wdn_ref[cols, :]) for i in slabs]
    out = [_layer_norm(ALPHA * h1[i] + acc[i], ln_ref[2:3, :], ln_ref[3:4, :]) for i in slabs]
    return jnp.concatenate(out, axis=0)


def _proj_kernel(x_ref, xprev_ref, head_ref, mu_ref, vec_ref, wr_ref, wk_ref, wv_ref,
                 w1_ref, w2_ref, a1_ref, a2_ref, g1_ref, g2_ref, gsum_ref, gexp_ref,
                 h_out, r_out, lw_out, k_out, v_out, kk_out, b_out, g_out, bonus_out):
    first_tile = pl.program_id(1) == 0
    xw = x_ref[0]
    tm = xw.shape[0]
    x = jnp.where(first_tile, jnp.concatenate([head_ref[...], xw[:tm - BLOCK]], axis=0), xw)
    row = lax.broadcasted_iota(jnp.int32, x.shape, 0)
    prev_row = jnp.where(first_tile, 0.0, xprev_ref[0, 7:8, :])
    x_prev = jnp.where(row == 0, prev_row, pltpu.roll(x, 1, axis=0))
    xx = x_prev - x
    h_out[0] = x

    def mix(i):
        return (x + xx * mu_ref[i:i + 1, :]).astype(BF16)

    r = _dot(mix(0), wr_ref[...])
    k = _dot(mix(2), wk_ref[...])
    v = _dot(mix(3), wv_ref[...])
    w_lora = _dot(jnp.tanh(_dot(mix(1), w1_ref[...])).astype(BF16), w2_ref[...])
    a_lora = _dot(_dot(mix(4), a1_ref[...]).astype(BF16), a2_ref[...])
    g = _dot(jax.nn.sigmoid(_dot(mix(5), g1_ref[...])).astype(BF16), g2_ref[...])

    w0 = vec_ref[0:1, :]
    a0 = vec_ref[1:2, :]
    k_k = vec_ref[2:3, :]
    k_a = vec_ref[3:4, :]
    r_k = vec_ref[4:5, :]
    w = -jax.nn.softplus(-(w0 + w_lora)) - 0.5
    lw = -jnp.exp(w)
    a = jax.nn.sigmoid(a0 + a_lora)

    kk = k * k_k
    ss = _dot((kk * kk).astype(BF16), gsum_ref[...])
    inv = lax.rsqrt(jnp.maximum(ss, 1e-24))
    kk = kk * _dot(_split_bf16(inv), gexp_ref[...])
    k2 = k * (1.0 + (a - 1.0) * k_a)
    rk_sum = _dot((r * k2 * r_k).astype(BF16), gsum_ref[...])

    lw_out[0] = lw
    r_out[0] = r.astype(r_out.dtype)
    k_out[0] = k2.astype(k_out.dtype)
    v_out[0] = v.astype(v_out.dtype)
    kk_out[0] = kk.astype(kk_out.dtype)
    b_out[0] = (kk * a).astype(b_out.dtype)
    g_out[0] = g.astype(g_out.dtype)
    bonus_out[0] = (_dot(_split_bf16(rk_sum), gexp_ref[...]) * v).astype(bonus_out.dtype)


def _rwkv_proj(x, head, mu, vec, wr, wk, wv, w1, w2, a1, a2, g1, g2, gsum, gexp):
    bsz, seq, d = x.shape
    lp = seq + BLOCK
    tm = TA_ROWS
    const = lambda shape: pl.BlockSpec(shape, lambda bi, i: (0, 0), pipeline_mode=pl.Buffered(1))
    tile = pl.BlockSpec((1, tm, d), lambda bi, i: (bi, i, 0))
    x_rows = pl.BlockSpec(
        (pl.Element(1), pl.Element(tm), pl.Element(d)),
        lambda bi, i: (bi, pl.multiple_of(jnp.maximum(i * tm - BLOCK, 0), BLOCK), 0))
    x_prev_rows = pl.BlockSpec(
        (pl.Element(1), pl.Element(8), pl.Element(d)),
        lambda bi, i: (bi, pl.multiple_of(jnp.maximum(i * tm - BLOCK - 8, 0), 8), 0))
    f32_sds = jax.ShapeDtypeStruct((bsz, lp, d), F32)
    bf16_sds = jax.ShapeDtypeStruct((bsz, lp, d), BF16)
    return pl.pallas_call(
        _proj_kernel,
        grid=(bsz, lp // tm),
        in_specs=[
            x_rows, x_prev_rows, const(head.shape),
            const(mu.shape), const(vec.shape), const(wr.shape), const(wk.shape), const(wv.shape),
            const(w1.shape), const(w2.shape), const(a1.shape), const(a2.shape),
            const(g1.shape), const(g2.shape), const(gsum.shape), const(gexp.shape),
        ],
        out_specs=[tile] * 9,
        out_shape=[f32_sds, bf16_sds, f32_sds] + [bf16_sds] * 6,
        compiler_params=pltpu.CompilerParams(
            dimension_semantics=("parallel", "parallel"), vmem_limit_bytes=VMEM_LIMIT),
        name="rwkv_proj",
    )(x, x, head, mu, vec, wr, wk, wv, w1, w2, a1, a2, g1, g2, gsum, gexp)


def _chunk_kernel(r_ref, lw_ref, k_ref, v_ref, kk_ref, b_ref,
                  rw_out, y0_out, mlr_out, sadd_out, dec_out,
                  at_s, rt_s, kh_s, bh_s, kb_s, bb_s):
    C = CHUNK
    ta = lw_ref.shape[1]
    nch = ta // C

    lw = lw_ref[0]
    tri = (lax.broadcasted_iota(jnp.int32, (C, 2 * C), 0)
           >= (lax.broadcasted_iota(jnp.int32, (C, 2 * C), 1) & (C - 1))).astype(BF16)
    lw_hi = lw.astype(BF16)
    lw_lo = (lw - lw_hi.astype(F32)).astype(BF16)
    c = jnp.concatenate(
        [_dot(tri, jnp.concatenate([lw_hi[j * C:(j + 1) * C], lw_lo[j * C:(j + 1) * C]], axis=0))
         for j in range(nch)], axis=0)
    c3 = c.reshape(nch, C, D_MODEL)
    c_last = c3[:, C - 1:C, :]
    e_cc = jnp.exp(c_last - c3).reshape(ta, D_MODEL)
    dec_out[0] = jnp.exp(c_last)
    e_nc = jnp.exp(-c)
    k = k_ref[0].astype(F32)
    b = b_ref[0].astype(F32)
    at_s[...] = (-kk_ref[0].astype(F32) * jnp.exp(c - lw)).astype(BF16)
    rt_s[...] = (r_ref[0].astype(F32) * jnp.exp(c)).astype(BF16)
    kh_s[...] = (k * e_nc).astype(BF16)
    bh_s[...] = (b * e_nc).astype(BF16)
    kb_s[...] = (k * e_cc).astype(BF16)
    bb_s[...] = (b * e_cc).astype(BF16)

    t_idx = lax.broadcasted_iota(jnp.int32, (C, LANES), 0)
    s_idx = lax.broadcasted_iota(jnp.int32, (C, LANES), 1) & (HEAD_DIM - 1)
    strict = s_idx < t_idx
    incl = s_idx <= t_idx
    eye = (s_idx == t_idx).astype(F32)

    def chunk_body(j, carry):
        pairs = range(CHUNKS_PER_ITER * N_PAIRS)
        chunk = [CHUNKS_PER_ITER * j + u // N_PAIRS for u in pairs]
        rows = [pl.ds(pl.multiple_of(chunk[u] * C, C), C) for u in pairs]
        lanes = [slice((u % N_PAIRS) * LANES, (u % N_PAIRS + 1) * LANES) for u in pairs]
        at = [at_s[rows[p], lanes[p]] for p in pairs]
        rt = [rt_s[rows[p], lanes[p]] for p in pairs]
        ar = [jnp.concatenate([at[p], rt[p]], axis=0) for p in pairs]
        gk = [_dot_nt(ar[p], _block_diag(kh_s[rows[p], lanes[p]])) for p in pairs]
        gb = [_dot_nt(ar[p], _block_diag(bh_s[rows[p], lanes[p]])) for p in pairs]
        a_ak = [jnp.where(strict, gk[p][:C], 0.0).astype(BF16) for p in pairs]
        a_rk = [jnp.where(incl, gk[p][C:], 0.0).astype(BF16) for p in pairs]
        a_rb = [jnp.where(incl, gb[p][C:], 0.0).astype(BF16) for p in pairs]
        a_ab = [jnp.where(strict, gb[p][:C], 0.0) for p in pairs]
        tinv = [eye + a_ab[p] for p in pairs]
        npow = [a_ab[p].astype(BF16) for p in pairs]
        npow = [_dot(npow[p], _block_diag(npow[p])).astype(BF16) for p in pairs]
        for _ in range(4):
            res = [_dot(jnp.concatenate([npow[p], tinv[p].astype(BF16)], axis=0),
                        _block_diag(npow[p])) for p in pairs]
            tinv = [tinv[p] + res[p][C:] for p in pairs]
            npow = [res[p][:C].astype(BF16) for p in pairs]
        tinv = [(tinv[p] + _dot(tinv[p].astype(BF16), _block_diag(npow[p]))).astype(BF16)
                for p in pairs]
        vv = [v_ref[0, rows[p], lanes[p]] for p in pairs]
        x = [_dot(a_ak[p], _block_diag(vv[p])).astype(BF16) for p in pairs]
        wu = [_dot(tinv[p], jnp.concatenate([_block_diag(at[p]), _block_diag(x[p])], axis=1))
              for p in pairs]
        w_b = [wu[p][:, :LANES].astype(BF16) for p in pairs]
        u0_b = [wu[p][:, LANES:].astype(BF16) for p in pairs]
        for p in pairs:
            rw = rt[p].astype(F32) + _dot(a_rb[p], _block_diag(w_b[p]))
            rw_out[0, rows[p], lanes[p]] = rw.astype(rw_out.dtype)
        for p in pairs:
            y0_out[0, rows[p], lanes[p]] = _dot(
                jnp.concatenate([a_rk[p], a_rb[p]], axis=1),
                jnp.concatenate([_block_diag(vv[p]), _block_diag(u0_b[p])], axis=0))
        for p in pairs:
            bb = bb_s[rows[p], lanes[p]]
            mlr_out[0, chunk[p], :, lanes[p]] = _diag_blocks(
                _dot_tn(w_b[p], bb)).astype(mlr_out.dtype)
            sadd_out[0, chunk[p], :, lanes[p]] = _diag_blocks(
                _dot_tn(jnp.concatenate([vv[p], u0_b[p]], axis=0),
                        jnp.concatenate([kb_s[rows[p], lanes[p]], bb], axis=0)))
        return carry

    lax.fori_loop(0, nch // CHUNKS_PER_ITER, chunk_body, 0)


def _rwkv_chunk(r, lw, k, v, kk, b):
    bsz, lp, d = r.shape
    ta = TA_ROWS
    nch = ta // CHUNK
    nc = lp // CHUNK
    tile = pl.BlockSpec((1, ta, d), lambda bi, i: (bi, i, 0))
    per_chunk = lambda rows: pl.BlockSpec((1, nch, rows, d), lambda bi, i: (bi, i, 0, 0))
    return pl.pallas_call(
        _chunk_kernel,
        grid=(bsz, lp // ta),
        in_specs=[tile] * 6,
        out_specs=[tile, tile, per_chunk(HEAD_DIM), per_chunk(HEAD_DIM), per_chunk(1)],
        out_shape=[
            jax.ShapeDtypeStruct((bsz, lp, d), BF16),
            jax.ShapeDtypeStruct((bsz, lp, d), F32),
            jax.ShapeDtypeStruct((bsz, nc, HEAD_DIM, d), BF16),
            jax.ShapeDtypeStruct((bsz, nc, HEAD_DIM, d), F32),
            jax.ShapeDtypeStruct((bsz, nc, 1, d), F32),
        ],
        scratch_shapes=[pltpu.VMEM((ta, d), BF16)] * 6,
        compiler_params=pltpu.CompilerParams(
            dimension_semantics=("parallel", "parallel"), vmem_limit_bytes=VMEM_LIMIT),
        name="rwkv_chunk",
    )(r, lw, k, v, kk, b)


def _state_kernel(rw_ref, y0_ref, mlr_ref, sadd_ref, dec_ref, bonus_ref, g_ref, hv_ref,
                  gsum_ref, gexp_ref, o_ref, s_ref, y_s):
    bsz = rw_ref.shape[0]

    @pl.when(pl.program_id(0) == 0)
    def _():
        s_ref[...] = jnp.zeros_like(s_ref)

    def seq_body(i, carry):
        for u in range(SEQS_PER_ITER * N_PAIRS):
            bi = SEQS_PER_ITER * i + u // N_PAIRS
            ls = slice((u % N_PAIRS) * LANES, (u % N_PAIRS + 1) * LANES)
            s0 = s_ref[bi, :, ls]
            s0b = s0.astype(BF16)
            y_s[bi, :, ls] = y0_ref[bi, :, ls] + _dot_nt(rw_ref[bi, :, ls], _block_diag(s0b))
            s_ref[bi, :, ls] = (s0 * dec_ref[bi, 0, :, ls]
                                + _dot(s0b, _block_diag(mlr_ref[bi, 0, :, ls]))
                                + sadd_ref[bi, 0, :, ls])
        return carry

    lax.fori_loop(0, bsz // SEQS_PER_ITER, seq_body, 0)

    nb = bsz // ROW_SLABS
    slabs = range(ROW_SLABS)
    rows2d = lambda ref, i: ref[i * nb:(i + 1) * nb].reshape(nb * CHUNK, D_MODEL)
    inv_n = 1.0 / HEAD_DIM
    y = [rows2d(y_s, i) for i in slabs]
    s1 = [_dot(y[i].astype(BF16), gsum_ref[...]) * inv_n for i in slabs]
    yc = [y[i] - _dot(_split_bf16(s1[i]), gexp_ref[...]) for i in slabs]
    var = [_dot((yc[i] * yc[i]).astype(BF16), gsum_ref[...]) * inv_n for i in slabs]
    rstd = [_dot(_split_bf16(lax.rsqrt(var[i] + GN_EPS)), gexp_ref[...]) for i in slabs]
    for i in slabs:
        yn = yc[i] * rstd[i] * hv_ref[0:1, :] + hv_ref[1:2, :]
        o = (yn + rows2d(bonus_ref, i).astype(F32)) * rows2d(g_ref, i).astype(F32)
        o_ref[i * nb:(i + 1) * nb] = o.astype(o_ref.dtype).reshape(nb, CHUNK, D_MODEL)


def _rwkv_state(rw, y0, mlr, sadd, dec, bonus, g, hv, gsum, gexp):
    bsz, lp, d = rw.shape
    nc = lp // CHUNK
    tile = pl.BlockSpec((bsz, CHUNK, d), lambda n: (0, n, 0))
    per_chunk = lambda rows: pl.BlockSpec((bsz, 1, rows, d), lambda n: (0, n, 0, 0))
    const = lambda shape: pl.BlockSpec(shape, lambda n: (0, 0))
    return pl.pallas_call(
        _state_kernel,
        grid=(nc,),
        in_specs=[tile, tile, per_chunk(HEAD_DIM), per_chunk(HEAD_DIM), per_chunk(1), tile, tile,
                  const(hv.shape), const(gsum.shape), const(gexp.shape)],
        out_specs=tile,
        out_shape=jax.ShapeDtypeStruct((bsz, lp, d), BF16),
        scratch_shapes=[pltpu.VMEM((bsz, HEAD_DIM, d), F32), pltpu.VMEM((bsz, CHUNK, d), F32)],
        compiler_params=pltpu.CompilerParams(
            dimension_semantics=("arbitrary",), vmem_limit_bytes=VMEM_LIMIT),
        name="rwkv_state",
    )(rw, y0, mlr, sadd, dec, bonus, g, hv, gsum, gexp)


def _post_mlp_kernel(o_ref, h_ref, wo_ref, ln_ref, wup_ref, wdn_ref, out_ref):
    out_ref[...] = _deepnorm_mlp(h_ref[...], o_ref[...], wo_ref, ln_ref, wup_ref, wdn_ref)


def _post_mlp(o_flat, h_flat, wo, ln, wup, wdn):
    rows = h_flat.shape[0]
    tm = TM_ROWS
    const = lambda shape: pl.BlockSpec(shape, lambda i: (0, 0), pipeline_mode=pl.Buffered(1))
    tile = pl.BlockSpec((tm, D_MODEL), lambda i: (i, 0))
    return pl.pallas_call(
        _post_mlp_kernel,
        grid=(rows // tm,),
        in_specs=[tile, tile, const(wo.shape), const(ln.shape), const(wup.shape), const(wdn.shape)],
        out_specs=tile,
        out_shape=jax.ShapeDtypeStruct((rows, D_MODEL), F32),
        compiler_params=pltpu.CompilerParams(
            dimension_semantics=("parallel",), vmem_limit_bytes=VMEM_LIMIT),
        name="post_mlp",
    )(o_flat, h_flat, wo, ln, wup, wdn)


def _rope(t, cos, sin_signed):
    out = []
    lane = lax.broadcasted_iota(jnp.int32, (t.shape[0], LANES), 1)
    first_half = (lane % HEAD_DIM) < (HEAD_DIM // 2)
    for j in range(t.shape[1] // LANES):
        tj = t[:, j * LANES:(j + 1) * LANES]
        rot = jnp.where(first_half, pltpu.roll(tj, LANES - HEAD_DIM // 2, axis=1),
                        pltpu.roll(tj, HEAD_DIM // 2, axis=1))
        out.append(tj * cos + rot * sin_signed)
    return jnp.concatenate(out, axis=1)


def _swa_kernel(sink_ref, h_ref, hprev_ref, cos_ref, sin_ref, cosp_ref, sinp_ref,
                wq_ref, wkv_ref, wo_ref, ln_ref, wup_ref, wdn_ref, out_ref, o_scr):
    h = h_ref[0]
    tm = h.shape[0]
    hb = h.astype(BF16)
    hcat = jnp.concatenate([hprev_ref[0].astype(BF16), hb], axis=0)
    cos_cat = jnp.concatenate([cosp_ref[...], cos_ref[...]], axis=0)
    sin_cat = jnp.concatenate([sinp_ref[...], sin_ref[...]], axis=0)

    nq_rows = tm // ROW_SLABS
    nk_rows = (tm + BLOCK) // ROW_SLABS
    qp = [_dot(hb[i * nq_rows:(i + 1) * nq_rows], wq_ref[...]) for i in range(ROW_SLABS)]
    kvp = [_dot(hcat[i * nk_rows:(i + 1) * nk_rows], wkv_ref[...]) for i in range(ROW_SLABS)]
    q = jnp.concatenate(
        [_rope(qp[i], cos_ref[i * nq_rows:(i + 1) * nq_rows], sin_ref[i * nq_rows:(i + 1) * nq_rows])
         for i in range(ROW_SLABS)], axis=0) * (HEAD_DIM ** -0.5)
    kv = jnp.concatenate(kvp, axis=0)
    kdim = N_HEADS_KV * HEAD_DIM

    def twice(t):
        half = lax.broadcasted_iota(jnp.int32, (t.shape[0], LANES), 1) < HEAD_DIM
        out = []
        for j in range(t.shape[1] // LANES):
            tj = t[:, j * LANES:(j + 1) * LANES]
            sw = pltpu.roll(tj, HEAD_DIM, axis=1)
            out += [jnp.where(half, tj, sw), jnp.where(half, sw, tj)]
        return jnp.concatenate(out, axis=1)

    kr = twice(_rope(kv[:, :kdim], cos_cat, sin_cat)).astype(BF16)
    vb = twice(kv[:, kdim:]).astype(BF16)
    qb = q.astype(BF16)

    nq = GROUP * BLOCK
    first_row = BLOCK + pl.program_id(1) * tm
    qi = lax.broadcasted_iota(jnp.int32, (nq, 2 * BLOCK), 0) & (BLOCK - 1)
    kj = lax.broadcasted_iota(jnp.int32, (nq, 2 * BLOCK), 1)
    rel = BLOCK + qi - kj
    in_window = (rel >= 0) & (rel < WINDOW)
    head_of_row = lax.broadcasted_iota(jnp.int32, (nq, 1), 0) // BLOCK
    first = lax.broadcasted_iota(jnp.int32, (BLOCK, LANES), 1) < HEAD_DIM
    zero = jnp.zeros((BLOCK, LANES), BF16)

    groups = range(N_HEADS_KV)
    for j in range(tm // BLOCK):
        rows = slice(j * BLOCK, (j + 1) * BLOCK)
        keys = slice(j * BLOCK, (j + 2) * BLOCK)
        key_row = first_row + (j - 1) * BLOCK + kj
        mask = in_window & (key_row >= PAD_FRONT)
        scores = []
        for g in groups:
            qa = qb[rows, 2 * g * LANES:(2 * g + 1) * LANES]
            qc = qb[rows, (2 * g + 1) * LANES:(2 * g + 2) * LANES]
            qs = jnp.concatenate([jnp.where(first, qa, zero), jnp.where(first, zero, qa),
                                  jnp.where(first, qc, zero), jnp.where(first, zero, qc)], axis=0)
            scores.append(_dot_nt(qs, kr[keys, g * LANES:(g + 1) * LANES]))
        probs, denoms = [], []
        for g in groups:
            s = jnp.where(mask, scores[g], -jnp.inf)
            sink = jnp.full((nq, 1), sink_ref[GROUP * g], F32)
            for i in range(1, GROUP):
                sink = jnp.where(head_of_row == i, sink_ref[GROUP * g + i], sink)
            m = jnp.maximum(jnp.max(s, axis=-1, keepdims=True), sink)
            p = jnp.exp(s - m)
            denoms.append(jnp.sum(p, axis=-1, keepdims=True) + jnp.exp(sink - m))
            probs.append(p.astype(BF16))
        for g in groups:
            o4 = _dot(probs[g], vb[keys, g * LANES:(g + 1) * LANES]) / denoms[g]
            o_scr[rows, 2 * g * LANES:(2 * g + 1) * LANES] = jnp.where(
                first, o4[:BLOCK], o4[BLOCK:2 * BLOCK]).astype(BF16)
            o_scr[rows, (2 * g + 1) * LANES:(2 * g + 2) * LANES] = jnp.where(
                first, o4[2 * BLOCK:3 * BLOCK], o4[3 * BLOCK:]).astype(BF16)

    out_ref[0] = _deepnorm_mlp(h, o_scr[...], wo_ref, ln_ref, wup_ref, wdn_ref)


def _swa_layer(h, sinks, cos, sin_signed, wq, wkv, wo, ln, wup, wdn):
    bsz, lp, d = h.shape
    tm = TM_ATT
    nblk = tm // BLOCK
    seq = lp - BLOCK
    const = lambda shape: pl.BlockSpec(shape, lambda bi, i: (0, 0), pipeline_mode=pl.Buffered(1))
    h_tile = pl.BlockSpec((pl.Element(1), pl.Element(tm), pl.Element(d)),
                          lambda bi, i: (bi, pl.multiple_of(BLOCK + i * tm, BLOCK), 0))
    table_tile = pl.BlockSpec((pl.Element(tm), pl.Element(LANES)),
                              lambda bi, i: (pl.multiple_of(BLOCK + i * tm, BLOCK), 0))
    return pl.pallas_call(
        _swa_kernel,
        grid=(bsz, seq // tm),
        in_specs=[
            pl.BlockSpec(memory_space=pltpu.SMEM),
            h_tile,
            pl.BlockSpec((1, BLOCK, d), lambda bi, i: (bi, i * nblk, 0)),
            table_tile,
            table_tile,
            pl.BlockSpec((BLOCK, LANES), lambda bi, i: (i * nblk, 0)),
            pl.BlockSpec((BLOCK, LANES), lambda bi, i: (i * nblk, 0)),
            const(wq.shape), const(wkv.shape), const(wo.shape), const(ln.shape),
            const(wup.shape), const(wdn.shape),
        ],
        out_specs=pl.BlockSpec((1, tm, d), lambda bi, i: (bi, i, 0)),
        out_shape=jax.ShapeDtypeStruct((bsz, seq, d), F32),
        scratch_shapes=[pltpu.VMEM((tm, d), BF16)],
        compiler_params=pltpu.CompilerParams(
            dimension_semantics=("parallel", "parallel"), vmem_limit_bytes=VMEM_LIMIT),
        name="swa_layer",
    )(sinks, h, h, cos, sin_signed, cos, sin_signed, wq, wkv, wo, ln, wup, wdn)


def kernel(x, meta_tokens, a_mu, a_w_r, a_w_k, a_w_v, a_w_o, a_w0, a_w1, a_w2, a_a0, a_a1, a_a2,
           a_g1, a_g2, a_k_k, a_k_a, a_r_k, a_gn_w, a_gn_b, kv_w_k, kv_w_v, b_w_q, b_sinks, b_w_o,
           mlp_w_up, mlp_w_down, ln_g, ln_b):
    bsz, seq, d = x.shape
    lp = PAD_FRONT + N_META + seq
    bf = lambda t: t.astype(BF16)

    head = jnp.concatenate([jnp.zeros((PAD_FRONT, d), x.dtype), meta_tokens.astype(x.dtype)], axis=0)

    head_of_lane = jnp.arange(d) // HEAD_DIM
    gsum = (head_of_lane[:, None] == jnp.arange(LANES)[None, :]).astype(BF16)
    gexp = jnp.concatenate([gsum.T, gsum.T], axis=0)

    vec = jnp.stack([a_w0[0], a_a0[0], a_k_k[0], a_k_a[0], a_r_k[0].reshape(d)])
    h0, r, lw, k, v, kk, b, g, bonus = _rwkv_proj(
        x, head, a_mu[0], vec, bf(a_w_r[0]), bf(a_w_k[0]), bf(a_w_v[0]),
        bf(a_w1[0]), bf(a_w2[0]), bf(a_a1[0]), bf(a_a2[0]), bf(a_g1[0]), bf(a_g2[0]), gsum, gexp)

    shp = (bsz, lp, d)
    rw, y0, mlr, sadd, dec = _rwkv_chunk(r, lw, k, v, kk, b)
    hv = jnp.stack([a_gn_w[0], a_gn_b[0]])
    o = _rwkv_state(rw, y0, mlr, sadd, dec, bonus, g, hv, gsum, gexp)

    ln0 = jnp.stack([ln_g[0, 0], ln_b[0, 0], ln_g[0, 1], ln_b[0, 1]])
    h1 = _post_mlp(o.reshape(bsz * lp, d), h0.reshape(bsz * lp, d), bf(a_w_o[0]), ln0,
                   bf(mlp_w_up[0]), bf(mlp_w_down[0]))

    pos = (jnp.arange(lp) - PAD_FRONT).astype(F32)
    inv_freq = 1.0 / (ROPE_THETA ** (jnp.arange(0, HEAD_DIM, 2, dtype=F32) / HEAD_DIM))
    ang = pos[:, None] * inv_freq[None, :]
    cos = jnp.tile(jnp.cos(ang), (1, 4))
    sin_signed = jnp.tile(jnp.concatenate([-jnp.sin(ang), jnp.sin(ang)], axis=1), (1, 2))

    ln1 = jnp.stack([ln_g[1, 0], ln_b[1, 0], ln_g[1, 1], ln_b[1, 1]])
    wkv = bf(jnp.concatenate([kv_w_k, kv_w_v], axis=1))
    return _swa_layer(h1.reshape(shp), b_sinks[0], cos, sin_signed, bf(b_w_q[0]), wkv, bf(b_w_o[0]),
                      ln1, bf(mlp_w_up[1]), bf(mlp_w_down[1]))
```
